```python
import math
import jax, jax.numpy as jnp
from jax import lax
import numpy as np

D_MODEL = 1024
BATCH = 8
SEQ = 2048
DEPTH = 4
DEC_BATCH = 128
DEC_SEQ = 1
PAST_LEN = 2048
PAGE_SIZE = 128

N_MIXERS = 3
LAYER_MIXER = tuple(i % N_MIXERS for i in range(DEPTH))
LAYER_SLOT = tuple(LAYER_MIXER[:i].count(LAYER_MIXER[i]) for i in range(DEPTH))
N_CONV_LAYERS = LAYER_MIXER.count(0)
N_HGRN_LAYERS = LAYER_MIXER.count(1)
N_MOBA_LAYERS = LAYER_MIXER.count(2)

CONV_W = 3
HG_HEADS = D_MODEL // 128
HG_KDIM = 128
HG_VDIM = 128
HG_CHUNK = 64
MB_HEADS = D_MODEL // 128
MB_HDIM = 128
MB_BLOCK = 256
MB_TOPK = 3
MB_QCHUNK = 16
ROPE_THETA = 10000.0
D_FF = 2816
FFN_CONV_W = 3
EPS = 1e-6

kernel_name = 'hybrid_conv_hgrn2_moba_decoder_step'


def rms_norm(x, g):
    xf = x.astype(jnp.float32)
    y = xf * lax.rsqrt(jnp.mean(xf * xf, axis=-1, keepdims=True) + EPS)
    return (y * g.astype(jnp.float32)).astype(x.dtype)


def causal_dwconv(u, buf, w):
    L = u.shape[1]
    ext = jnp.concatenate([buf.astype(u.dtype), u], axis=1)
    out = w[0] * ext[:, 0:L]
    for tap in range(1, w.shape[0]):
        out = out + w[tap] * ext[:, tap:tap + L]
    return out, ext[:, L:]


def rope(x, pos):
    half = x.shape[-1] // 2
    inv = jnp.exp(-math.log(ROPE_THETA) * jnp.arange(half, dtype=jnp.float32) / half)
    ang = pos.astype(jnp.float32)[:, None] * inv[None, :]
    cos = jnp.cos(ang)[None, :, None, :]
    sin = jnp.sin(ang)[None, :, None, :]
    xf = x.astype(jnp.float32)
    x1, x2 = xf[..., :half], xf[..., half:]
    return jnp.concatenate([x1 * cos - x2 * sin, x2 * cos + x1 * sin], axis=-1).astype(x.dtype)


def short_conv_mixer(h, buf, w_in, w_conv, w_out):
    b_gate, c_gate, val = jnp.split(h @ w_in, 3, axis=-1)
    conv, new_buf = causal_dwconv(c_gate * val, buf, w_conv)
    return (b_gate * conv) @ w_out, new_buf


def _to_chunks(a, C, n):
    bsz, L = a.shape[:2]
    a = jnp.pad(a, ((0, 0), (0, n * C - L), (0, 0), (0, 0)))
    return a.reshape(bsz, n, C, a.shape[2], a.shape[3]).transpose(1, 0, 3, 2, 4)


def _hgrn_chunk(S, xs):
    q, kk, v, lf = xs
    C = q.shape[2]
    cum = jnp.cumsum(lf, axis=2)
    o_inter = jnp.einsum('bhck,bhkv->bhcv', q * jnp.exp(cum), S)
    causal = jnp.tril(jnp.ones((C, C), dtype=bool))
    diff = cum[:, :, :, None, :] - cum[:, :, None, :, :]
    decay = jnp.exp(jnp.where(causal[:, :, None], diff, -jnp.inf))
    attn = jnp.einsum('bhtk,bhsk,bhtsk->bhts', q, kk, decay)
    o = o_inter + jnp.einsum('bhts,bhsv->bhtv', attn, v)
    last = cum[:, :, -1:, :]
    S_new = jnp.exp(last[:, :, 0, :])[..., None] * S + jnp.einsum('bhsk,bhsv->bhkv', kk * jnp.exp(last - cum), v)
    return S_new, o


def hgrn2_mixer(h, s0, w_in, lb, g_norm, w_out):
    bsz, L, _ = h.shape
    f32 = jnp.float32
    q, fz, i_in, g = jnp.split(h @ w_in, 4, axis=-1)
    kshape = (bsz, L, HG_HEADS, HG_KDIM)
    lbf = lb.astype(f32).reshape(HG_HEADS, HG_KDIM)
    fz = fz.astype(f32).reshape(kshape)
    log_f = jnp.log(lbf + (1.0 - lbf) * jax.nn.sigmoid(fz))
    k_in = (1.0 - lbf) * jax.nn.sigmoid(-fz)
    q = jax.nn.silu(q.astype(f32)).reshape(kshape)
    v = i_in.astype(f32).reshape(bsz, L, HG_HEADS, HG_VDIM)
    C = min(HG_CHUNK, L)
    n = -(-L // C)
    xs = (_to_chunks(q, C, n), _to_chunks(k_in, C, n), _to_chunks(v, C, n), _to_chunks(log_f, C, n))
    s_fin, o = lax.scan(_hgrn_chunk, s0.astype(f32), xs)
    o = o.transpose(1, 0, 3, 2, 4).reshape(bsz, n * C, HG_HEADS, HG_VDIM)[:, :L]
    o = rms_norm(o, g_norm).reshape(bsz, L, HG_HEADS * HG_VDIM).astype(h.dtype)
    return (o * jax.nn.silu(g)) @ w_out, s_fin.astype(s0.dtype)


def _moba_layout(q_start, L):
    qc = min(MB_QCHUNK, L)
    n_chunks = -(-L // qc)
    n_blk = -(-(q_start + n_chunks * qc) // MB_BLOCK)
    return qc, n_chunks, n_blk


def _kv_rows(new, cache, slot, page_table, total):
    bsz = new.shape[0]
    tail = new.shape[2:]
    parts = [new] if cache is None else [cache[slot, page_table].reshape((bsz, -1) + tail).astype(new.dtype), new]
    n_pad = total - sum(p.shape[1] for p in parts)
    return jnp.concatenate(parts + [jnp.zeros((bsz, n_pad) + tail, new.dtype)], axis=1)


def moba_attention(q, k_full, v_full, q_start, qc, n_chunks):
    bsz, L, H, Dh = q.shape
    f32 = jnp.float32
    n_blk = k_full.shape[1] // MB_BLOCK
    n_cand = max(n_blk, MB_TOPK)
    scale = Dh ** -0.5
    q = jnp.pad(q, ((0, 0), (0, n_chunks * qc - L), (0, 0), (0, 0)))
    k_blocks = k_full.reshape(bsz, n_blk, MB_BLOCK, H, Dh)
    v_blocks = v_full.reshape(bsz, n_blk, MB_BLOCK, H, Dh)
    means = jnp.mean(k_blocks.astype(f32), axis=2)
    means = jnp.pad(means, ((0, 0), (0, n_cand - n_blk), (0, 0), (0, 0)))
    bi = jnp.arange(bsz)[:, None, None, None]
    hi = jnp.arange(H)[None, :, None, None]
    cand = jnp.arange(n_cand)
    rank = jnp.arange(MB_TOPK)
    offs = jnp.arange(MB_BLOCK)

    def one_chunk(ci):
        p0 = q_start + ci * qc
        cur = p0 // MB_BLOCK
        qb = lax.dynamic_slice_in_dim(q, ci * qc, qc, axis=1)
        qpos = p0 + jnp.arange(qc)
        gate = jnp.einsum('bqhd,bnhd->bhqn', qb.astype(f32), means)
        gate = jnp.where(cand < cur, gate, -jnp.inf)
        idx = jnp.minimum(lax.top_k(gate, MB_TOPK)[1], n_blk - 1)
        k_sel = k_blocks[bi, idx, :, hi]
        v_sel = v_blocks[bi, idx, :, hi]
        s_sel = jnp.einsum('bqhd,bhqjpd->bhqjp', qb, k_sel, preferred_element_type=f32) * scale
        s_sel = jnp.where((rank < cur)[:, None], s_sel, -jnp.inf)
        own0 = cur * MB_BLOCK
        k_own = lax.dynamic_slice_in_dim(k_full, own0, MB_BLOCK, axis=1)
        v_own = lax.dynamic_slice_in_dim(v_full, own0, MB_BLOCK, axis=1)
        s_own = jnp.einsum('bqhd,bphd->bhqp', qb, k_own, preferred_element_type=f32) * scale
        s_own = jnp.where((own0 + offs)[None, :] <= qpos[:, None], s_own, -jnp.inf)
        logits = jnp.concatenate([s_sel.reshape(bsz, H, qc, MB_TOPK * MB_BLOCK), s_own], axis=-1)
        p = jax.nn.softmax(logits, axis=-1).astype(v_full.dtype)
        p_sel = p[..., :MB_TOPK * MB_BLOCK].reshape(bsz, H, qc, MB_TOPK, MB_BLOCK)
        p_own = p[..., MB_TOPK * MB_BLOCK:]
        return jnp.einsum('bhqjp,bhqjpd->bqhd', p_sel, v_sel) + jnp.einsum('bhqp,bphd->bqhd', p_own, v_own)

    o = lax.map(one_chunk, jnp.arange(n_chunks))
    return o.transpose(1, 0, 2, 3, 4).reshape(bsz, n_chunks * qc, H, Dh)[:, :L]


def moba_mixer(h, q_start, cache_k, cache_v, slot, page_table, w_qkv, q_norm, k_norm, w_out):
    bsz, L, _ = h.shape
    q, k, v = jnp.split(h @ w_qkv, 3, axis=-1)
    shp = (bsz, L, MB_HEADS, MB_HDIM)
    pos = q_start + jnp.arange(L, dtype=jnp.int32)
    q = rope(rms_norm(q.reshape(shp), q_norm), pos)
    k = rope(rms_norm(k.reshape(shp), k_norm), pos)
    v = v.reshape(shp)
    qc, n_chunks, n_blk = _moba_layout(q_start, L)
    k_full = _kv_rows(k, cache_k, slot, page_table, n_blk * MB_BLOCK)
    v_full = _kv_rows(v, cache_v, slot, page_table, n_blk * MB_BLOCK)
    o = moba_attention(q, k_full, v_full, q_start, qc, n_chunks)
    return o.reshape(bsz, L, MB_HEADS * MB_HDIM) @ w_out, k, v


def conv_ffn(h, buf, w_up, w_conv, b_conv, w_down):
    a, gate = jnp.split(h @ w_up, 2, axis=-1)
    a_c, new_buf = causal_dwconv(a, buf, w_conv)
    return (jax.nn.silu(a_c + b_conv) * gate) @ w_down, new_buf


def setup_inputs(seed: int = 0) -> dict:
    key = jax.random.key(seed)
    ks = iter(jax.random.split(key, 32))
    f32 = jnp.float32
    D = D_MODEL

    def nrm(shape, scale):
        return jax.random.normal(next(ks), shape, f32) * scale

    n_pages = PAST_LEN // PAGE_SIZE
    n_used = DEC_BATCH * n_pages
    n_phys = (5 * n_used) // 4
    page_table = jax.random.permutation(next(ks), n_phys)[:n_used].reshape(DEC_BATCH, n_pages).astype(jnp.int32)
    return {
        'x_prompt': nrm((BATCH, SEQ, D), 1.0),
        'x_sample': nrm((DEC_BATCH, DEC_SEQ, D), 1.0),
        'state_shortconv': nrm((N_CONV_LAYERS, DEC_BATCH, CONV_W - 1, D), 1.0),
        'state_hgrn': nrm((N_HGRN_LAYERS, DEC_BATCH, HG_HEADS, HG_KDIM, HG_VDIM), 0.5),
        'cache_k': nrm((N_MOBA_LAYERS, n_phys, PAGE_SIZE, MB_HEADS, MB_HDIM), 1.0),
        'cache_v': nrm((N_MOBA_LAYERS, n_phys, PAGE_SIZE, MB_HEADS, MB_HDIM), 1.0),
        'page_table': page_table,
        'state_ffn_conv': nrm((DEPTH, DEC_BATCH, FFN_CONV_W - 1, D_FF), 1.0),
        'norm_mix': 1.0 + nrm((DEPTH, D), 0.02),
        'norm_ffn': 1.0 + nrm((DEPTH, D), 0.02),
        'w_in_a': nrm((N_CONV_LAYERS, D, 3 * D), D ** -0.5),
        'w_conv_a': nrm((N_CONV_LAYERS, CONV_W, D), CONV_W ** -0.5),
        'w_out_a': nrm((N_CONV_LAYERS, D, D), D ** -0.5),
        'w_in_b': nrm((N_HGRN_LAYERS, D, 4 * HG_HEADS * HG_KDIM), D ** -0.5),
        'lb_raw': nrm((DEPTH, HG_HEADS * HG_KDIM), 0.5),
        'g_norm_b': 1.0 + nrm((N_HGRN_LAYERS, HG_VDIM), 0.02),
        'w_out_b': nrm((N_HGRN_LAYERS, HG_HEADS * HG_VDIM, D), (HG_HEADS * HG_VDIM) ** -0.5),
        'w_qkv_c': nrm((N_MOBA_LAYERS, D, 3 * MB_HEADS * MB_HDIM), D ** -0.5),
        'q_norm_c': 1.0 + nrm((N_MOBA_LAYERS, MB_HDIM), 0.02),
        'k_norm_c': 1.0 + nrm((N_MOBA_LAYERS, MB_HDIM), 0.02),
        'w_out_c': nrm((N_MOBA_LAYERS, MB_HEADS * MB_HDIM, D), (MB_HEADS * MB_HDIM) ** -0.5),
        'w_up': nrm((DEPTH, D, 2 * D_FF), D ** -0.5),
        'w_ffn_conv': nrm((DEPTH, FFN_CONV_W, D_FF), FFN_CONV_W ** -0.5),
        'b_ffn_conv': nrm((DEPTH, D_FF), 0.01),
        'w_down': nrm((DEPTH, D_FF, D), D_FF ** -0.5),
    }


def reference(x_prompt, x_sample, state_shortconv, state_hgrn, cache_k, cache_v, page_table, state_ffn_conv,
              norm_mix, norm_ffn, w_in_a, w_conv_a, w_out_a, w_in_b, lb_raw, g_norm_b, w_out_b,
              w_qkv_c, q_norm_c, k_norm_c, w_out_c, w_up, w_ffn_conv, b_ffn_conv, w_down):
    xp, xs = x_prompt, x_sample
    bp = xp.shape[0]
    past_len = page_table.shape[1] * cache_k.shape[2]
    lb_w = jax.nn.softmax(lb_raw.astype(jnp.float32), axis=0)
    lower_bounds = jnp.cumsum(lb_w, axis=0) - lb_w[0]
    sc_p, sc_s, hg_p, hg_s, kp_l, vp_l, ks_l, vs_l, fc_p, fc_s = ([] for _ in range(10))
    for i in range(DEPTH):
        mixer, j = LAYER_MIXER[i], LAYER_SLOT[i]
        hp = rms_norm(xp, norm_mix[i])
        hs = rms_norm(xs, norm_mix[i])
        if mixer == 0:
            buf0 = jnp.zeros((bp, CONV_W - 1, D_MODEL), state_shortconv.dtype)
            yp, bp_new = short_conv_mixer(hp, buf0, w_in_a[j], w_conv_a[j], w_out_a[j])
            ys, bs_new = short_conv_mixer(hs, state_shortconv[j], w_in_a[j], w_conv_a[j], w_out_a[j])
            sc_p.append(bp_new)
            sc_s.append(bs_new)
        elif mixer == 1:
            s0 = jnp.zeros((bp, HG_HEADS, HG_KDIM, HG_VDIM), state_hgrn.dtype)
            yp, sp_new = hgrn2_mixer(hp, s0, w_in_b[j], lower_bounds[i], g_norm_b[j], w_out_b[j])
            ys, ss_new = hgrn2_mixer(hs, state_hgrn[j], w_in_b[j], lower_bounds[i], g_norm_b[j], w_out_b[j])
            hg_p.append(sp_new)
            hg_s.append(ss_new)
        else:
            yp, kp, vp = moba_mixer(hp, 0, None, None, None, None, w_qkv_c[j], q_norm_c[j], k_norm_c[j], w_out_c[j])
            ys, kn, vn = moba_mixer(hs, past_len, cache_k, cache_v, j, page_table, w_qkv_c[j], q_norm_c[j], k_norm_c[j], w_out_c[j])
            kp_l.append(kp)
            vp_l.append(vp)
            ks_l.append(kn)
            vs_l.append(vn)
        xp = xp + yp
        xs = xs + ys
        hp = rms_norm(xp, norm_ffn[i])
        hs = rms_norm(xs, norm_ffn[i])
        fbuf0 = jnp.zeros((bp, FFN_CONV_W - 1, D_FF), state_ffn_conv.dtype)
        yp, fp_new = conv_ffn(hp, fbuf0, w_up[i], w_ffn_conv[i], b_ffn_conv[i], w_down[i])
        ys, fs_new = conv_ffn(hs, state_ffn_conv[i], w_up[i], w_ffn_conv[i], b_ffn_conv[i], w_down[i])
        fc_p.append(fp_new)
        fc_s.append(fs_new)
        xp = xp + yp
        xs = xs + ys
    return (xp, xs, jnp.stack(sc_p), jnp.stack(sc_s), jnp.stack(hg_p), jnp.stack(hg_s),
            jnp.stack(kp_l), jnp.stack(vp_l), jnp.stack(ks_l), jnp.stack(vs_l), jnp.stack(fc_p), jnp.stack(fc_s))
```

```python
import functools
import math

import jax
import jax.numpy as jnp
from jax import lax
from jax.experimental import pallas as pl
from jax.experimental.pallas import tpu as pltpu

F32 = jnp.float32
BF16 = jnp.bfloat16

EPS = 1e-6
HEAD_DIM = 128
CONV_TAPS = 3
HG_CHUNK = 64
HG_SUB = 16
MB_BLOCK = 256
MB_TOPK = 3
ROPE_THETA = 10000.0
V7X_VMEM_BYTES = 64 * 1024 * 1024
VMEM_LIMIT = V7X_VMEM_BYTES - 8 * 1024 * 1024

NT_DIMS = (((1,), (1,)), ((), ()))
TN_DIMS = (((0,), (0,)), ((), ()))


def _params(*sem):
    return pltpu.CompilerParams(dimension_semantics=sem, vmem_limit_bytes=VMEM_LIMIT)


def _const_spec(shape):
    n = len(shape)
    return pl.BlockSpec(shape, lambda *_: (0,) * n, pipeline_mode=pl.Buffered(1))


def _rms(x, g):
    return x * lax.rsqrt(jnp.mean(x * x, axis=-1, keepdims=True) + EPS) * g


def _sigmoid(x):
    return 1.0 / (1.0 + jnp.exp(-x))


def _silu(x):
    return x * _sigmoid(x)


def _bdot(a, b):
    return jnp.dot(a.astype(BF16), b.astype(BF16), preferred_element_type=F32)


def _bdot_nt(a, b):
    return lax.dot_general(a.astype(BF16), b.astype(BF16), NT_DIMS, preferred_element_type=F32)


def _gcb_kernel(*refs, ffn, decode, tiles_per_seq, width, cw):
    refs = list(refs)
    x_ref, g_ref, w1_ref, wc_ref = refs[:4]
    refs = refs[4:]
    bc_ref = refs.pop(0) if ffn else None
    w2_ref = refs.pop(0)
    if decode:
        sm2_ref, sm1_ref, out_ref, unew_ref = refs
    else:
        out_ref, st_ref, carry_ref = refs
    x = x_ref[...]
    tm = x.shape[0]
    h = _rms(x, g_ref[...]).astype(BF16)
    if not decode:
        @pl.when(pl.program_id(0) % tiles_per_seq == 0)
        def _():
            carry_ref[...] = jnp.zeros_like(carry_ref)
        row = lax.broadcasted_iota(jnp.int32, (tm, cw), 0)
    out_ref[...] = x
    for c in range(width // cw):
        sl = pl.ds(c * cw, cw)

        def proj(j):
            return jnp.dot(h, w1_ref[:, pl.ds(j * width + c * cw, cw)], preferred_element_type=F32)

        if ffn:
            u, gate = proj(0), proj(1)
        else:
            bg = proj(0)
            u = proj(1) * proj(2)
        if decode:
            um2, um1 = sm2_ref[:, sl], sm1_ref[:, sl]
            unew_ref[:, sl] = u
        else:
            c0, c1 = carry_ref[0:1, sl], carry_ref[1:2, sl]
            um1 = jnp.where(row == 0, c1, pltpu.roll(u, 1, axis=0))
            um2 = jnp.where(row == 0, c0, jnp.where(row == 1, c1, pltpu.roll(u, 2, axis=0)))
            carry_ref[0:2, sl] = u[tm - 2:tm, :]
        wc = wc_ref[:, sl]
        conv = wc[0:1] * um2 + wc[1:2] * um1 + wc[2:3] * u
        z = _silu(conv + bc_ref[:, sl]) * gate if ffn else bg * conv
        out_ref[...] += jnp.dot(z.astype(BF16), w2_ref[sl, :], preferred_element_type=F32)
    if not decode:
        st_ref[...] = carry_ref[0:2, :]


def _gcb_tile(rows):
    for tm in (512, 256, 128, 64, 32, 16, 8):
        if rows % tm == 0:
            return tm
    raise ValueError(f"row count {rows} must be a multiple of 8")


def gated_conv_block_prompt(x, g, w1, wc, bc, w2, *, seq):
    m, d = x.shape
    width = w2.shape[0]
    ffn = bc is not None
    tm = _gcb_tile(seq)
    tps = seq // tm
    cw = 256
    args = [x, g.reshape(1, d), w1, wc] + ([bc.reshape(1, width)] if ffn else []) + [w2]
    in_specs = [pl.BlockSpec((tm, d), lambda i: (i, 0)), _const_spec((1, d)), _const_spec(w1.shape),
                _const_spec(wc.shape)] + ([_const_spec((1, width))] if ffn else []) + [_const_spec(w2.shape)]
    return pl.pallas_call(
        functools.partial(_gcb_kernel, ffn=ffn, decode=False, tiles_per_seq=tps, width=width, cw=cw),
        grid=(m // tm,),
        in_specs=in_specs,
        out_specs=[pl.BlockSpec((tm, d), lambda i: (i, 0)),
                   pl.BlockSpec((None, CONV_TAPS - 1, width), lambda i: (i // tps, 0, 0))],
        out_shape=[jax.ShapeDtypeStruct((m, d), F32),
                   jax.ShapeDtypeStruct((m // seq, CONV_TAPS - 1, width), F32)],
        scratch_shapes=[pltpu.VMEM((8, width), F32)],
        compiler_params=_params("arbitrary"),
    )(*args)


def gated_conv_block_decode(x, g, w1, wc, bc, w2, state):
    m, d = x.shape
    width = w2.shape[0]
    ffn = bc is not None
    cw = 256
    args = [x, g.reshape(1, d), w1, wc] + ([bc.reshape(1, width)] if ffn else []) + [w2, state[:, 0], state[:, 1]]
    in_specs = [_const_spec(a.shape) for a in args]
    out, unew = pl.pallas_call(
        functools.partial(_gcb_kernel, ffn=ffn, decode=True, tiles_per_seq=1, width=width, cw=cw),
        grid=(1,),
        in_specs=in_specs,
        out_specs=[pl.BlockSpec((m, d), lambda i: (0, 0)), pl.BlockSpec((m, width), lambda i: (0, 0))],
        out_shape=[jax.ShapeDtypeStruct((m, d), F32), jax.ShapeDtypeStruct((m, width), F32)],
        compiler_params=_params("arbitrary"),
    )(*args)
    return out, jnp.stack([state[:, 1], unew], axis=1)


def _proj_residual_kernel(z_ref, w_ref, x_ref, out_ref):
    acc = x_ref[...]
    for hd in range(z_ref.shape[0]):
        acc = acc + _bdot(z_ref[hd], w_ref[hd])
    out_ref[...] = acc


def proj_residual_heads(z, w, x):
    nh, m, hd = z.shape
    d = x.shape[1]
    tm = _gcb_tile(m)
    return pl.pallas_call(
        _proj_residual_kernel,
        grid=(m // tm,),
        in_specs=[pl.BlockSpec((nh, tm, hd), lambda i: (0, i, 0)), _const_spec(w.shape),
                  pl.BlockSpec((tm, d), lambda i: (i, 0))],
        out_specs=pl.BlockSpec((tm, d), lambda i: (i, 0)),
        out_shape=jax.ShapeDtypeStruct((m, d), F32),
        compiler_params=_params("arbitrary"),
    )(z, w, x)


def _proj_residual_flat_kernel(z_ref, w_ref, x_ref, out_ref):
    out_ref[...] = x_ref[...] + _bdot(z_ref[...], w_ref[...])


def proj_residual(z, w, x):
    m, k = z.shape
    d = x.shape[1]
    tm = _gcb_tile(m)
    return pl.pallas_call(
        _proj_residual_flat_kernel,
        grid=(m // tm,),
        in_specs=[pl.BlockSpec((tm, k), lambda i: (i, 0)), _const_spec(w.shape),
                  pl.BlockSpec((tm, d), lambda i: (i, 0))],
        out_specs=pl.BlockSpec((tm, d), lambda i: (i, 0)),
        out_shape=jax.ShapeDtypeStruct((m, d), F32),
        compiler_params=_params("arbitrary"),
    )(z, w, x)


def _lower_bound(rows, layer):
    mx = functools.reduce(jnp.maximum, rows)
    e = [jnp.exp(r - mx) for r in rows]
    tot = functools.reduce(jnp.add, e)
    w = [a / tot for a in e]
    cum = w[0]
    for j in range(1, layer + 1):
        cum = cum + w[j]
    return cum - w[0]


def _split3(x):
    hi = x.astype(BF16)
    r = x - hi.astype(F32)
    mid = r.astype(BF16)
    lo = (r - mid.astype(F32)).astype(BF16)
    return hi, mid, lo


def _hgrn_chunk(q, k, v, lf, st, tri, ones):
    C = q.shape[0]
    nsub = C // HG_SUB
    hi, mid, lo = _split3(lf)
    cum = (jnp.dot(tri, hi, preferred_element_type=F32) + jnp.dot(tri, mid, preferred_element_type=F32)
           + jnp.dot(tri, lo, preferred_element_type=F32))
    o_inter = _bdot_nt(q * jnp.exp(cum), st)
    trow = lax.broadcasted_iota(jnp.int32, (HG_SUB, HEAD_DIM), 0)
    blocks = []
    for i in range(nsub):
        r0 = i * HG_SUB
        qb, kb, vb, cb = (a[r0:r0 + HG_SUB] for a in (q, k, v, cum))
        ob = o_inter[r0:r0 + HG_SUB]
        if i > 0:
            ref = cum[r0 - 1:r0]
            qi = qb * jnp.exp(cb - ref)
            ki = k[0:r0] * jnp.exp(ref - cum[0:r0])
            ob = ob + _bdot(_bdot_nt(qi, ki), v[0:r0])
        for s in range(HG_SUB):
            w = jnp.where(trow >= s, qb * jnp.exp(cb - cb[s:s + 1]) * kb[s:s + 1], 0.0)
            ob = ob + jnp.dot(w.astype(BF16), ones, preferred_element_type=F32) * vb[s:s + 1]
        blocks.append(ob)
    last = cum[C - 1:C]
    kd = k * jnp.exp(last - cum)
    upd = lax.dot_general(v.astype(BF16), kd.astype(BF16), TN_DIMS, preferred_element_type=F32)
    return jnp.concatenate(blocks, axis=0), st * jnp.exp(last) + upd


def _hgrn_gates(fz, lower):
    f = lower + (1.0 - lower) * _sigmoid(fz)
    k = (1.0 - lower) * _sigmoid(-fz)
    return f, k


def _hgrn_prompt_kernel(x_ref, g_ref, w_ref, lb_ref, gn_ref, wo_ref, out_ref, st_ref,
                        s_ref, lo_ref, q_s, k_s, v_s, lf_s, gg_s, z_s, *, layer, tiles_per_seq):
    nh = w_ref.shape[1]
    tm = x_ref.shape[0]
    C = HG_CHUNK
    t = pl.program_id(0)

    @pl.when(t % tiles_per_seq == 0)
    def _():
        s_ref[...] = jnp.zeros_like(s_ref)

    x = x_ref[...]
    hb = _rms(x, g_ref[...]).astype(BF16)
    lo_ref[...] = _lower_bound([lb_ref[j] for j in range(lb_ref.shape[0])], layer)
    tri = jnp.where(lax.broadcasted_iota(jnp.int32, (C, C), 0) >= lax.broadcasted_iota(jnp.int32, (C, C), 1),
                    1.0, 0.0).astype(BF16)
    ones = jnp.ones((HEAD_DIM, HEAD_DIM), BF16)
    gn = gn_ref[...]

    def head_body(hd, carry):
        lower = lo_ref[pl.ds(hd, 1), :]
        fz = jnp.dot(hb, w_ref[1, hd], preferred_element_type=F32)
        f, kk = _hgrn_gates(fz, lower)
        lf_s[...] = jnp.log(f)
        k_s[...] = kk
        q_s[...] = _silu(jnp.dot(hb, w_ref[0, hd], preferred_element_type=F32))
        v_s[...] = jnp.dot(hb, w_ref[2, hd], preferred_element_type=F32)
        gg_s[...] = jnp.dot(hb, w_ref[3, hd], preferred_element_type=F32)

        def chunk_body(ci, c2):
            rows = pl.ds(pl.multiple_of(ci * C, C), C)
            o, st_new = _hgrn_chunk(q_s[rows, :], k_s[rows, :], v_s[rows, :], lf_s[rows, :], s_ref[hd], tri, ones)
            s_ref[hd] = st_new
            z_s[hd, rows, :] = (_rms(o, gn) * _silu(gg_s[rows, :])).astype(BF16)
            return c2

        lax.fori_loop(0, tm // C, chunk_body, 0)
        return carry

    lax.fori_loop(0, nh, head_body, 0)
    acc = x
    for hd in range(nh):
        acc = acc + jnp.dot(z_s[hd], wo_ref[hd], preferred_element_type=F32)
    out_ref[...] = acc

    @pl.when(t % tiles_per_seq == tiles_per_seq - 1)
    def _():
        for hd in range(nh):
            st_ref[hd] = s_ref[hd].T


def hgrn_prompt(x, g, w_in4, lb_raw3, g_norm, w_out3, *, layer, seq):
    m, d = x.shape
    nh = w_in4.shape[1]
    tm = 256 if seq % 256 == 0 else HG_CHUNK
    assert seq % tm == 0
    tps = seq // tm
    return pl.pallas_call(
        functools.partial(_hgrn_prompt_kernel, layer=layer, tiles_per_seq=tps),
        grid=(m // tm,),
        in_specs=[pl.BlockSpec((tm, d), lambda i: (i, 0)), _const_spec((1, d)), _const_spec(w_in4.shape),
                  _const_spec(lb_raw3.shape), _const_spec((1, HEAD_DIM)), _const_spec(w_out3.shape)],
        out_specs=[pl.BlockSpec((tm, d), lambda i: (i, 0)),
                   pl.BlockSpec((None, nh, HEAD_DIM, HEAD_DIM), lambda i: (i // tps, 0, 0, 0))],
        out_shape=[jax.ShapeDtypeStruct((m, d), F32),
                   jax.ShapeDtypeStruct((m // seq, nh, HEAD_DIM, HEAD_DIM), F32)],
        scratch_shapes=[pltpu.VMEM((nh, HEAD_DIM, HEAD_DIM), F32), pltpu.VMEM((nh, HEAD_DIM), F32)]
        + [pltpu.VMEM((tm, HEAD_DIM), F32)] * 5 + [pltpu.VMEM((nh, tm, HEAD_DIM), BF16)],
        compiler_params=_params("arbitrary"),
    )(x, g.reshape(1, d), w_in4, lb_raw3, g_norm.reshape(1, HEAD_DIM), w_out3)


def _head_proj_kernel(x_ref, g_ref, w_ref, out_ref):
    out_ref[...] = jnp.dot(_rms(x_ref[...], g_ref[...]).astype(BF16), w_ref[...], preferred_element_type=F32)


def head_proj(x, g, w4):
    m, d = x.shape
    ng, nh = w4.shape[:2]
    return pl.pallas_call(
        _head_proj_kernel,
        grid=(ng, nh),
        in_specs=[_const_spec((m, d)), _const_spec((1, d)),
                  pl.BlockSpec((None, None, d, HEAD_DIM), lambda j, h: (j, h, 0, 0))],
        out_specs=pl.BlockSpec((None, None, m, HEAD_DIM), lambda j, h: (j, h, 0, 0)),
        out_shape=jax.ShapeDtypeStruct((ng, nh, m, HEAD_DIM), F32),
        compiler_params=_params("arbitrary", "arbitrary"),
    )(x, g.reshape(1, d), w4)


def _lane_replicate(rows, eye, ones):
    lhs = []
    for r in rows:
        hi = jnp.where(eye, r, 0.0).astype(BF16)
        lo = jnp.where(eye, r - r.astype(BF16).astype(F32), 0.0).astype(BF16)
        lhs += [hi, lo]
    rep = jnp.dot(jnp.concatenate(lhs, axis=0), ones, preferred_element_type=F32)
    n = HEAD_DIM
    return [rep[2 * i * n:(2 * i + 1) * n] + rep[(2 * i + 1) * n:(2 * i + 2) * n] for i in range(len(rows))]


def _hgrn_decode_kernel(p_ref, s_ref, lb_ref, gn_ref, z_ref, sn_ref, *, layer):
    tb = s_ref.shape[0]
    lower = _lower_bound([lb_ref[pl.ds(j, 1), :] for j in range(lb_ref.shape[0])], layer)
    f_all, k_all = _hgrn_gates(p_ref[1], lower)
    q_all = _silu(p_ref[0])
    eye = lax.broadcasted_iota(jnp.int32, (HEAD_DIM, HEAD_DIM), 0) == lax.broadcasted_iota(
        jnp.int32, (HEAD_DIM, HEAD_DIM), 1)
    ones = jnp.ones((HEAD_DIM, HEAD_DIM), BF16)
    gn = gn_ref[...]
    for b in range(tb):
        sl = slice(b, b + 1)
        fr, kr, qr = _lane_replicate([f_all[sl], k_all[sl], q_all[sl]], eye, ones)
        s_new = fr * s_ref[b] + kr * p_ref[2, sl, :]
        sn_ref[b] = s_new
        o = jnp.sum(qr * s_new, axis=0, keepdims=True)
        z_ref[sl, :] = _rms(o, gn) * _silu(p_ref[3, sl, :])


def hgrn_decode(proj, state, lb_raw3, g_norm, *, layer):
    _, nh, bd, _ = proj.shape
    tb = 8
    assert bd % tb == 0
    depth = lb_raw3.shape[0]
    return pl.pallas_call(
        functools.partial(_hgrn_decode_kernel, layer=layer),
        grid=(nh, bd // tb),
        in_specs=[pl.BlockSpec((4, None, tb, HEAD_DIM), lambda h, t: (0, h, t, 0)),
                  pl.BlockSpec((tb, None, HEAD_DIM, HEAD_DIM), lambda h, t: (t, h, 0, 0)),
                  pl.BlockSpec((None, depth, HEAD_DIM), lambda h, t: (h, 0, 0)),
                  pl.BlockSpec((1, HEAD_DIM), lambda h, t: (0, 0))],
        out_specs=[pl.BlockSpec((None, tb, HEAD_DIM), lambda h, t: (h, t, 0)),
                   pl.BlockSpec((tb, None, HEAD_DIM, HEAD_DIM), lambda h, t: (t, h, 0, 0))],
        out_shape=[jax.ShapeDtypeStruct((nh, bd, HEAD_DIM), F32), jax.ShapeDtypeStruct(state.shape, F32)],
        compiler_params=_params("arbitrary", "arbitrary"),
    )(proj, state, lb_raw3.transpose(1, 0, 2), g_norm.reshape(1, HEAD_DIM))


def _rope_tables(pos):
    half = HEAD_DIM // 2
    inv = jnp.exp(-math.log(ROPE_THETA) * jnp.arange(half, dtype=F32) / half)
    ang = pos.astype(F32)[:, None] * inv[None, :]
    cos, sin = jnp.cos(ang), jnp.sin(ang)
    return jnp.concatenate([cos, cos], axis=-1), jnp.concatenate([-sin, sin], axis=-1)


def _qkv_kernel(x_ref, g_ref, w_ref, qn_ref, kn_ref, cos_ref, sin_ref, q_ref, k_ref, v_ref, km_ref):
    d = x_ref.shape[1]
    hb = _rms(x_ref[...], g_ref[...]).astype(BF16)
    cos, sin = cos_ref[...], sin_ref[...]

    def norm_rope(y, gain):
        y = _rms(y, gain)
        return y * cos + pltpu.roll(y, HEAD_DIM // 2, axis=1) * sin

    for hd in range(d // HEAD_DIM):
        sl = pl.ds(hd * HEAD_DIM, HEAD_DIM)
        q = jnp.dot(hb, w_ref[:, pl.ds(hd * HEAD_DIM, HEAD_DIM)], preferred_element_type=F32)
        k = jnp.dot(hb, w_ref[:, pl.ds(d + hd * HEAD_DIM, HEAD_DIM)], preferred_element_type=F32)
        q_ref[:, sl] = norm_rope(q, qn_ref[...])
        k = norm_rope(k, kn_ref[...])
        k_ref[:, sl] = k
        km_ref[:, sl] = jnp.mean(k, axis=0, keepdims=True)
        v_ref[:, sl] = jnp.dot(hb, w_ref[:, pl.ds(2 * d + hd * HEAD_DIM, HEAD_DIM)], preferred_element_type=F32)


def moba_qkv(x, g, w, qn, kn, cosf, sinf, *, tm):
    m, d = x.shape
    tps = cosf.shape[0] // tm
    return pl.pallas_call(
        _qkv_kernel,
        grid=(m // tm,),
        in_specs=[pl.BlockSpec((tm, d), lambda i: (i, 0)), _const_spec((1, d)), _const_spec(w.shape),
                  _const_spec((1, HEAD_DIM)), _const_spec((1, HEAD_DIM)),
                  pl.BlockSpec((tm, HEAD_DIM), lambda i: (i % tps, 0)),
                  pl.BlockSpec((tm, HEAD_DIM), lambda i: (i % tps, 0))],
        out_specs=[pl.BlockSpec((tm, d), lambda i: (i, 0))] * 3 + [pl.BlockSpec((None, 1, d), lambda i: (i, 0, 0))],
        out_shape=[jax.ShapeDtypeStruct((m, d), F32)] * 3 + [jax.ShapeDtypeStruct((m // tm, 1, d), F32)],
        compiler_params=_params("arbitrary"),
    )(x, g.reshape(1, d), w, qn.reshape(1, HEAD_DIM), kn.reshape(1, HEAD_DIM), cosf, sinf)


def _top_blocks(gate, n_idx, n_valid):
    neg = -jnp.inf
    g = jnp.where(n_idx < n_valid, gate, neg)
    sel = jnp.zeros(gate.shape, F32)
    big = jnp.int32(2 ** 30)
    for _ in range(MB_TOPK):
        m = jnp.max(g, axis=1, keepdims=True)
        idx = jnp.min(jnp.where(g == m, n_idx, big), axis=1, keepdims=True)
        hit = n_idx == idx
        sel = jnp.where(hit, jnp.where(m > neg, 1.0, sel), sel)
        g = jnp.where(hit, neg, g)
    return sel


def _moba_prompt_kernel(q_ref, k_ref, v_ref, km_ref, o_ref, *, scale):
    blk = q_ref.shape[0]
    nblk = km_ref.shape[0]
    i = pl.program_id(2)
    q = q_ref[...]
    km = jnp.concatenate([km_ref[...], jnp.zeros((HEAD_DIM - nblk, HEAD_DIM), F32)], axis=0)
    gate = lax.dot_general(q, km, NT_DIMS, precision=lax.Precision.HIGHEST, preferred_element_type=F32)
    n_idx = lax.broadcasted_iota(jnp.int32, (blk, HEAD_DIM), 1)
    sel = _top_blocks(gate, n_idx, i)
    qb = q.astype(BF16)
    own = pl.ds(pl.multiple_of(i * blk, blk), blk)
    s = _bdot_nt(qb, k_ref[own, :]) * scale
    causal = lax.broadcasted_iota(jnp.int32, (blk, blk), 1) <= lax.broadcasted_iota(jnp.int32, (blk, blk), 0)
    s = jnp.where(causal, s, -jnp.inf)
    m0 = jnp.max(s, axis=1, keepdims=True)
    p = jnp.exp(s - m0)
    l0 = jnp.sum(p, axis=1, keepdims=True)
    acc0 = _bdot(p, v_ref[own, :])

    def body(n, carry):
        m, l, acc = carry
        rows = pl.ds(pl.multiple_of(n * blk, blk), blk)
        take = jnp.sum(jnp.where(n_idx == n, sel, 0.0), axis=1, keepdims=True) > 0.5
        sn = jnp.where(take, _bdot_nt(qb, k_ref[rows, :]) * scale, -jnp.inf)
        m_new = jnp.maximum(m, jnp.max(sn, axis=1, keepdims=True))
        alpha = jnp.exp(m - m_new)
        pn = jnp.exp(sn - m_new)
        return m_new, alpha * l + jnp.sum(pn, axis=1, keepdims=True), alpha * acc + _bdot(pn, v_ref[rows, :])

    _, l, acc = lax.fori_loop(0, i, body, (m0, l0, acc0))
    o_ref[...] = acc / l


def moba_prompt_attention(q, k, v, kmeans, *, batch, seq):
    m, d = q.shape
    nh = d // HEAD_DIM
    nblk = seq // MB_BLOCK
    return pl.pallas_call(
        functools.partial(_moba_prompt_kernel, scale=HEAD_DIM ** -0.5),
        grid=(batch, nh, nblk),
        in_specs=[pl.BlockSpec((MB_BLOCK, HEAD_DIM), lambda b, h, i: (b * nblk + i, h)),
                  pl.BlockSpec((seq, HEAD_DIM), lambda b, h, i: (b, h)),
                  pl.BlockSpec((seq, HEAD_DIM), lambda b, h, i: (b, h)),
                  pl.BlockSpec((None, nblk, HEAD_DIM), lambda b, h, i: (b, 0, h))],
        out_specs=pl.BlockSpec((MB_BLOCK, HEAD_DIM), lambda b, h, i: (b * nblk + i, h)),
        out_shape=jax.ShapeDtypeStruct((m, d), F32),
        compiler_params=_params("arbitrary", "arbitrary", "arbitrary"),
    )(q, k, v, kmeans)


def _moba_decode_kernel(pt_ref, q_ref, kn_ref, vn_ref, ka_ref, kb_ref, va_ref, vb_ref, o_ref,
                        s_ref, mx_ref, sel_ref, g_ref, m_ref, l_ref, acc_ref, *, scale, nblk, rows):
    del pt_ref
    nh = q_ref.shape[0]
    j = pl.program_id(1)
    page_rows = ka_ref.shape[0]
    q = q_ref[...]
    lane = lax.broadcasted_iota(jnp.int32, (nh, HEAD_DIM), 1)
    ones = jnp.ones((HEAD_DIM, HEAD_DIM), BF16)
    nchunk = page_rows // rows

    def group_sum(a):
        return jnp.sum(a.reshape(rows // nh, nh, HEAD_DIM), axis=0)

    def group_max(a):
        return jnp.max(a.reshape(rows // nh, nh, HEAD_DIM), axis=0)

    def tile(a):
        return jnp.broadcast_to(a[None], (rows // nh, nh, HEAD_DIM)).reshape(rows, HEAD_DIM)

    @pl.when(j == 0)
    def _():
        g_ref[...] = jnp.zeros_like(g_ref)

    @pl.when(j < nblk)
    def _():
        ksum = jnp.zeros((nh, HEAD_DIM), F32)
        mx = jnp.full((nh, HEAD_DIM), -jnp.inf, F32)
        for e, ref in enumerate((ka_ref, kb_ref)):
            for c in range(nchunk):
                kc = ref[pl.ds(c * rows, rows), :]
                sc = jnp.dot((kc * tile(q)).astype(BF16), ones, preferred_element_type=F32) * scale
                s_ref[j, pl.ds(e * page_rows + c * rows, rows), :] = sc
                ksum = ksum + group_sum(kc)
                mx = jnp.maximum(mx, group_max(sc))
        mx_ref[j] = mx
        mean = ksum / (2 * page_rows // nh)
        gate = jnp.sum(q * mean, axis=1, keepdims=True)
        g_ref[...] = jnp.where(lane == j, gate, g_ref[...])

    @pl.when(j == nblk)
    def _():
        sel = _top_blocks(g_ref[...], lane, nblk)
        s_own = jnp.sum(q * kn_ref[...], axis=1, keepdims=True) * scale
        m = jnp.broadcast_to(s_own, (nh, HEAD_DIM))
        for n in range(nblk):
            sn = jnp.broadcast_to(jnp.sum(jnp.where(lane == n, sel, 0.0), axis=1, keepdims=True), (nh, HEAD_DIM))
            sel_ref[n] = sn
            m = jnp.maximum(m, jnp.where(sn > 0.5, mx_ref[n], -jnp.inf))
        m_ref[...] = m
        p_own = jnp.exp(s_own - m)
        l_ref[...] = p_own
        acc_ref[...] = p_own * vn_ref[...]

    @pl.when(j >= nblk)
    def _():
        n = j - nblk
        take = tile(sel_ref[n]) > 0.5
        mt = tile(m_ref[...])
        l = l_ref[...]
        acc = acc_ref[...]
        for e, ref in enumerate((va_ref, vb_ref)):
            for c in range(nchunk):
                sc = s_ref[n, pl.ds(e * page_rows + c * rows, rows), :]
                p = jnp.where(take, jnp.exp(sc - mt), 0.0)
                l = l + group_sum(p)
                acc = acc + group_sum(p * ref[pl.ds(c * rows, rows), :])
        l_ref[...] = l
        acc_ref[...] = acc

        @pl.when(j == 2 * nblk - 1)
        def _():
            o_ref[...] = acc / l


def moba_decode_attention(q, k_new, v_new, cache_k, cache_v, slot, page_table):
    bd, nh, _ = q.shape
    n_layers, n_phys, page = cache_k.shape[:3]
    n_pages = page_table.shape[1]
    assert 2 * page == MB_BLOCK and n_pages % 2 == 0
    nblk = n_pages // 2
    page_rows = page * nh
    ck = cache_k.reshape(n_layers, n_phys, page_rows, HEAD_DIM)
    cv = cache_v.reshape(n_layers, n_phys, page_rows, HEAD_DIM)

    def kmap(e):
        return lambda b, j, pt: (slot, pt[b, 2 * jnp.minimum(j, nblk - 1) + e], 0, 0)

    def vmap_(e):
        return lambda b, j, pt: (slot, pt[b, 2 * jnp.maximum(j - nblk, 0) + e], 0, 0)

    page_block = (None, None, page_rows, HEAD_DIM)
    vec = pl.BlockSpec((None, nh, HEAD_DIM), lambda b, j, pt: (b, 0, 0))
    small = pltpu.VMEM((nh, HEAD_DIM), F32)
    grid_spec = pltpu.PrefetchScalarGridSpec(
        num_scalar_prefetch=1,
        grid=(bd, 2 * nblk),
        in_specs=[vec, vec, vec, pl.BlockSpec(page_block, kmap(0)), pl.BlockSpec(page_block, kmap(1)),
                  pl.BlockSpec(page_block, vmap_(0)), pl.BlockSpec(page_block, vmap_(1))],
        out_specs=vec,
        scratch_shapes=[pltpu.VMEM((nblk, 2 * page_rows, HEAD_DIM), F32), pltpu.VMEM((nblk, nh, HEAD_DIM), F32),
                        pltpu.VMEM((nblk, nh, HEAD_DIM), F32), small, small, small, small],
    )
    return pl.pallas_call(
        functools.partial(_moba_decode_kernel, scale=HEAD_DIM ** -0.5, nblk=nblk, rows=256),
        grid_spec=grid_spec,
        out_shape=jax.ShapeDtypeStruct((bd, nh, HEAD_DIM), F32),
        compiler_params=_params("arbitrary", "arbitrary"),
    )(page_table, q, k_new, v_new, ck, ck, cv, cv)


def kernel(x_prompt, x_sample, state_shortconv, state_hgrn, cache_k, cache_v, page_table, state_ffn_conv, norm_mix, norm_ffn, w_in_a, w_conv_a, w_out_a, w_in_b, lb_raw, g_norm_b, w_out_b, w_qkv_c, q_norm_c, k_norm_c, w_out_c, w_up, w_ffn_conv, b_ffn_conv, w_down):
    bp, seq, d = x_prompt.shape
    bd = x_sample.shape[0]
    assert x_sample.shape[1] == 1
    depth = norm_mix.shape[0]
    nh = d // HEAD_DIM
    past_len = page_table.shape[1] * cache_k.shape[2]
    assert past_len % MB_BLOCK == 0 and seq % MB_BLOCK == 0

    xp = x_prompt.reshape(bp * seq, d)
    xs = x_sample.reshape(bd, d)
    lb_raw3 = lb_raw.reshape(depth, nh, HEAD_DIM)
    outs = {name: [] for name in ("sc_p", "sc_s", "hg_p", "hg_s", "kp", "vp", "ks", "vs", "fc_p", "fc_s")}
    n_conv = n_hgrn = n_moba = 0
    for i in range(depth):
        mixer = i % 3
        if mixer == 0:
            j, n_conv = n_conv, n_conv + 1
            w1, w2 = w_in_a[j].astype(BF16), w_out_a[j].astype(BF16)
            xp, st_p = gated_conv_block_prompt(xp, norm_mix[i], w1, w_conv_a[j], None, w2, seq=seq)
            xs, st_s = gated_conv_block_decode(xs, norm_mix[i], w1, w_conv_a[j], None, w2, state_shortconv[j])
            outs["sc_p"].append(st_p)
            outs["sc_s"].append(st_s)
        elif mixer == 1:
            j, n_hgrn = n_hgrn, n_hgrn + 1
            w4 = w_in_b[j].astype(BF16).reshape(d, 4, nh, HEAD_DIM).transpose(1, 2, 0, 3)
            wo3 = w_out_b[j].astype(BF16).reshape(nh, HEAD_DIM, d)
            xp, st_p = hgrn_prompt(xp, norm_mix[i], w4, lb_raw3, g_norm_b[j], wo3, layer=i, seq=seq)
            proj = head_proj(xs, norm_mix[i], w4)
            z, st_s = hgrn_decode(proj, state_hgrn[j], lb_raw3, g_norm_b[j], layer=i)
            xs = proj_residual_heads(z, wo3, xs)
            outs["hg_p"].append(st_p)
            outs["hg_s"].append(st_s)
        else:
            j, n_moba = n_moba, n_moba + 1
            wq, wo = w_qkv_c[j].astype(BF16), w_out_c[j].astype(BF16)
            cos_p, sin_p = _rope_tables(jnp.arange(seq, dtype=jnp.int32))
            q, k, v, km = moba_qkv(xp, norm_mix[i], wq, q_norm_c[j], k_norm_c[j], cos_p, sin_p, tm=MB_BLOCK)
            o = moba_prompt_attention(q, k, v, km.reshape(bp, seq // MB_BLOCK, d), batch=bp, seq=seq)
            xp = proj_residual(o, wo, xp)
            cos_s, sin_s = _rope_tables(jnp.full((bd,), past_len, jnp.int32))
            qs, ks, vs, _ = moba_qkv(xs, norm_mix[i], wq, q_norm_c[j], k_norm_c[j], cos_s, sin_s, tm=bd)
            os_ = moba_decode_attention(qs.reshape(bd, nh, HEAD_DIM), ks.reshape(bd, nh, HEAD_DIM),
                                        vs.reshape(bd, nh, HEAD_DIM), cache_k, cache_v, j, page_table)
            xs = proj_residual(os_.reshape(bd, d), wo, xs)
            outs["kp"].append(k.reshape(bp, seq, nh, HEAD_DIM))
            outs["vp"].append(v.reshape(bp, seq, nh, HEAD_DIM))
            outs["ks"].append(ks.reshape(bd, 1, nh, HEAD_DIM))
            outs["vs"].append(vs.reshape(bd, 1, nh, HEAD_DIM))
        w1, w2 = w_up[i].astype(BF16), w_down[i].astype(BF16)
        xp, st_p = gated_conv_block_prompt(xp, norm_ffn[i], w1, w_ffn_conv[i], b_ffn_conv[i], w2, seq=seq)
        xs, st_s = gated_conv_block_decode(xs, norm_ffn[i], w1, w_ffn_conv[i], b_ffn_conv[i], w2, state_ffn_conv[i])
        outs["fc_p"].append(st_p)
        outs["fc_s"].append(st_s)
    return (xp.reshape(bp, seq, d), xs.reshape(bd, 1, d), jnp.stack(outs["sc_p"]), jnp.stack(outs["sc_s"]),
            jnp.stack(outs["hg_p"]), jnp.stack(outs["hg_s"]), jnp.stack(outs["kp"]), jnp.stack(outs["vp"]),
            jnp.stack(outs["ks"]), jnp.stack(outs["vs"]), jnp.stack(outs["fc_p"]), jnp.stack(outs["fc_s"]))
```

```python
import functools
import math

import jax
import jax.numpy as jnp
from jax import lax
from jax.experimental import pallas as pl
from jax.experimental.pallas import tpu as pltpu

F32 = jnp.float32
BF16 = jnp.bfloat16

EPS = 1e-6
HEAD_DIM = 128
CONV_TAPS = 3
HG_CHUNK = 64
HG_SUB = 16
MB_BLOCK = 256
MB_TOPK = 3
ROPE_THETA = 10000.0
V7X_VMEM_BYTES = 64 * 1024 * 1024
VMEM_LIMIT = V7X_VMEM_BYTES - 8 * 1024 * 1024

NT_DIMS = (((1,), (1,)), ((), ()))
TN_DIMS = (((0,), (0,)), ((), ()))


def _params(*sem):
    return pltpu.CompilerParams(dimension_semantics=sem, vmem_limit_bytes=VMEM_LIMIT)


def _const_spec(shape):
    n = len(shape)
    return pl.BlockSpec(shape, lambda *_: (0,) * n, pipeline_mode=pl.Buffered(1))


def _rms(x, g):
    return x * lax.rsqrt(jnp.mean(x * x, axis=-1, keepdims=True) + EPS) * g


def _sigmoid(x):
    return 1.0 / (1.0 + jnp.exp(-x))


def _silu(x):
    return x * _sigmoid(x)


def _bdot(a, b):
    return jnp.dot(a.astype(BF16), b.astype(BF16), preferred_element_type=F32)


def _bdot_nt(a, b):
    return lax.dot_general(a.astype(BF16), b.astype(BF16), NT_DIMS, preferred_element_type=F32)


def _gcb_kernel(*refs, ffn, decode, tiles_per_seq, width, cw):
    refs = list(refs)
    x_ref, g_ref, w1_ref, wc_ref = refs[:4]
    refs = refs[4:]
    bc_ref = refs.pop(0) if ffn else None
    w2_ref = refs.pop(0)
    if decode:
        sm2_ref, sm1_ref, out_ref, unew_ref, z_ref = refs
    else:
        out_ref, st_ref, z_ref, carry_ref = refs
    x = x_ref[...]
    tm = x.shape[0]
    h = _rms(x, g_ref[...]).astype(BF16)
    if not decode:
        @pl.when(pl.program_id(0) % tiles_per_seq == 0)
        def _():
            carry_ref[...] = jnp.zeros_like(carry_ref)
        row = lax.broadcasted_iota(jnp.int32, (tm, cw), 0)
    for c in range(width // cw):
        sl = pl.ds(c * cw, cw)

        def proj(j):
            return jnp.dot(h, w1_ref[:, pl.ds(j * width + c * cw, cw)], preferred_element_type=F32)

        if ffn:
            u, gate = proj(0), proj(1)
        else:
            bg = proj(0)
            u = proj(1) * proj(2)
        if decode:
            um2, um1 = sm2_ref[:, sl], sm1_ref[:, sl]
            unew_ref[:, sl] = u
        else:
            c0, c1 = carry_ref[0:1, sl], carry_ref[1:2, sl]
            um1 = jnp.where(row == 0, c1, pltpu.roll(u, 1, axis=0))
            um2 = jnp.where(row == 0, c0, jnp.where(row == 1, c1, pltpu.roll(u, 2, axis=0)))
            carry_ref[0:2, sl] = u[tm - 2:tm, :]
        wc = wc_ref[:, sl]
        conv = wc[0:1] * um2 + wc[1:2] * um1 + wc[2:3] * u
        z = _silu(conv + bc_ref[:, sl]) * gate if ffn else bg * conv
        z_ref[:, sl] = z.astype(BF16)
    out_ref[...] = x + jnp.dot(z_ref[...], w2_ref[...], preferred_element_type=F32)
    if not decode:
        st_ref[...] = carry_ref[0:2, :]


def _gcb_tile(rows):
    for tm in (512, 256, 128, 64, 32, 16, 8):
        if rows % tm == 0:
            return tm
    raise ValueError(f"row count {rows} must be a multiple of 8")


def gated_conv_block_prompt(x, g, w1, wc, bc, w2, *, seq):
    m, d = x.shape
    width = w2.shape[0]
    ffn = bc is not None
    tm = _gcb_tile(seq)
    tps = seq // tm
    cw = 256
    args = [x, g.reshape(1, d), w1, wc] + ([bc.reshape(1, width)] if ffn else []) + [w2]
    in_specs = [pl.BlockSpec((tm, d), lambda i: (i, 0)), _const_spec((1, d)), _const_spec(w1.shape),
                _const_spec(wc.shape)] + ([_const_spec((1, width))] if ffn else []) + [_const_spec(w2.shape)]
    return pl.pallas_call(
        functools.partial(_gcb_kernel, ffn=ffn, decode=False, tiles_per_seq=tps, width=width, cw=cw),
        grid=(m // tm,),
        in_specs=in_specs,
        out_specs=[pl.BlockSpec((tm, d), lambda i: (i, 0)),
                   pl.BlockSpec((None, CONV_TAPS - 1, width), lambda i: (i // tps, 0, 0))],
        out_shape=[jax.ShapeDtypeStruct((m, d), F32),
                   jax.ShapeDtypeStruct((m // seq, CONV_TAPS - 1, width), F32)],
        scratch_shapes=[pltpu.VMEM((tm, width), BF16), pltpu.VMEM((8, width), F32)],
        compiler_params=_params("arbitrary"),
    )(*args)


def gated_conv_block_decode(x, g, w1, wc, bc, w2, state):
    m, d = x.shape
    width = w2.shape[0]
    ffn = bc is not None
    cw = 256
    args = [x, g.reshape(1, d), w1, wc] + ([bc.reshape(1, width)] if ffn else []) + [w2, state[:, 0], state[:, 1]]
    in_specs = [_const_spec(a.shape) for a in args]
    out, unew = pl.pallas_call(
        functools.partial(_gcb_kernel, ffn=ffn, decode=True, tiles_per_seq=1, width=width, cw=cw),
        grid=(1,),
        in_specs=in_specs,
        out_specs=[pl.BlockSpec((m, d), lambda i: (0, 0)), pl.BlockSpec((m, width), lambda i: (0, 0))],
        out_shape=[jax.ShapeDtypeStruct((m, d), F32), jax.ShapeDtypeStruct((m, width), F32)],
        scratch_shapes=[pltpu.VMEM((m, width), BF16)],
        compiler_params=_params("arbitrary"),
    )(*args)
    return out, jnp.stack([state[:, 1], unew], axis=1)


def _proj_residual_kernel(z_ref, w_ref, x_ref, out_ref):
    acc = x_ref[...]
    for hd in range(z_ref.shape[0]):
        acc = acc + _bdot(z_ref[hd], w_ref[hd])
    out_ref[...] = acc


def proj_residual_heads(z, w, x):
    nh, m, hd = z.shape
    d = x.shape[1]
    tm = _gcb_tile(m)
    return pl.pallas_call(
        _proj_residual_kernel,
        grid=(m // tm,),
        in_specs=[pl.BlockSpec((nh, tm, hd), lambda i: (0, i, 0)), _const_spec(w.shape),
                  pl.BlockSpec((tm, d), lambda i: (i, 0))],
        out_specs=pl.BlockSpec((tm, d), lambda i: (i, 0)),
        out_shape=jax.ShapeDtypeStruct((m, d), F32),
        compiler_params=_params("arbitrary"),
    )(z, w, x)


def _proj_residual_flat_kernel(z_ref, w_ref, x_ref, out_ref):
    out_ref[...] = x_ref[...] + _bdot(z_ref[...], w_ref[...])


def proj_residual(z, w, x):
    m, k = z.shape
    d = x.shape[1]
    tm = _gcb_tile(m)
    return pl.pallas_call(
        _proj_residual_flat_kernel,
        grid=(m // tm,),
        in_specs=[pl.BlockSpec((tm, k), lambda i: (i, 0)), _const_spec(w.shape),
                  pl.BlockSpec((tm, d), lambda i: (i, 0))],
        out_specs=pl.BlockSpec((tm, d), lambda i: (i, 0)),
        out_shape=jax.ShapeDtypeStruct((m, d), F32),
        compiler_params=_params("arbitrary"),
    )(z, w, x)


def _lower_bound(rows, layer):
    mx = functools.reduce(jnp.maximum, rows)
    e = [jnp.exp(r - mx) for r in rows]
    tot = functools.reduce(jnp.add, e)
    w = [a / tot for a in e]
    cum = w[0]
    for j in range(1, layer + 1):
        cum = cum + w[j]
    return cum - w[0]


def _split3(x):
    hi = x.astype(BF16)
    r = x - hi.astype(F32)
    mid = r.astype(BF16)
    lo = (r - mid.astype(F32)).astype(BF16)
    return hi, mid, lo


def _level_ref(cum, b):
    t = cum.shape[0]
    if b >= 4:
        c3 = cum.reshape(t // (2 * b), 2 * b, HEAD_DIM)
        return jnp.broadcast_to(c3[:, b - 1:b, :], c3.shape).reshape(t, HEAD_DIM)
    r = lax.broadcasted_iota(jnp.int32, cum.shape, 0) & (2 * b - 1)
    prev = pltpu.roll(cum, 1, axis=0)
    if b == 1:
        return jnp.where(r == 1, prev, cum)
    return jnp.where(r == 0, pltpu.roll(cum, t - 1, axis=0),
                     jnp.where(r == 1, cum, jnp.where(r == 2, prev, pltpu.roll(cum, 2, axis=0))))


def _hgrn_tile(q, k, v, cum, st, xor_ts, below):
    t = q.shape[0]
    o = _bdot_nt(q * jnp.exp(cum), st) + jnp.sum(q * k, axis=-1, keepdims=True) * v
    a = None
    b = t // 2
    while b >= 1:
        x = jnp.exp(-jnp.abs(cum - _level_ref(cum, b)))
        al = _bdot_nt(q * x, k * x)
        a = al if a is None else jnp.where(xor_ts < 2 * b, al, a)
        b //= 2
    o = o + _bdot(jnp.where(below, a, 0.0), v)
    last = cum[t - 1:t]
    upd = lax.dot_general(v.astype(BF16), (k * jnp.exp(last - cum)).astype(BF16), TN_DIMS,
                          preferred_element_type=F32)
    return o, st * jnp.exp(last) + upd


def _hgrn_gates(fz, lower):
    f = lower + (1.0 - lower) * _sigmoid(fz)
    k = (1.0 - lower) * _sigmoid(-fz)
    return f, k


def _hgrn_prompt_kernel(x_ref, g_ref, w_ref, lb_ref, gn_ref, wo_ref, out_ref, st_ref,
                        s_ref, q_s, k_s, v_s, gg_s, cum_s, z_s, *, layer, tiles_per_seq):
    tm, d = x_ref.shape
    nh = d // HEAD_DIM
    t = pl.program_id(0)

    @pl.when(t % tiles_per_seq == 0)
    def _():
        s_ref[...] = jnp.zeros_like(s_ref)

    def proj(j):
        return jnp.dot(hb, w_ref[:, pl.ds(j * d, d)], preferred_element_type=F32)

    x = x_ref[...]
    hb = _rms(x, g_ref[...]).astype(BF16)
    lower = _lower_bound([lb_ref[j:j + 1, :] for j in range(lb_ref.shape[0])], layer)
    f, kk = _hgrn_gates(proj(1), lower)
    k_s[...] = kk
    row = lax.broadcasted_iota(jnp.int32, (tm, tm), 0)
    col = lax.broadcasted_iota(jnp.int32, (tm, tm), 1)
    tri = jnp.where(row >= col, 1.0, 0.0).astype(BF16)
    cum_s[...] = functools.reduce(jnp.add, [jnp.dot(tri, part, preferred_element_type=F32)
                                            for part in _split3(jnp.log(f))])
    q_s[...] = _silu(proj(0))
    v_s[...] = proj(2)
    gg_s[...] = proj(3)
    xor_ts = row ^ col
    below = row > col
    gn = gn_ref[...]
    for hd in range(nh):
        sl = pl.ds(hd * HEAD_DIM, HEAD_DIM)
        o, st_new = _hgrn_tile(q_s[:, sl], k_s[:, sl], v_s[:, sl], cum_s[:, sl], s_ref[hd], xor_ts, below)
        s_ref[hd] = st_new
        z_s[:, sl] = (_rms(o, gn) * _silu(gg_s[:, sl])).astype(BF16)
    out_ref[...] = x + jnp.dot(z_s[...], wo_ref[...], preferred_element_type=F32)

    @pl.when(t % tiles_per_seq == tiles_per_seq - 1)
    def _():
        for hd in range(nh):
            st_ref[hd] = s_ref[hd].T


def hgrn_prompt(x, g, w_in, lb_raw, g_norm, w_out, *, layer, seq):
    m, d = x.shape
    nh = d // HEAD_DIM
    tm = min(_gcb_tile(seq), 256)
    tps = seq // tm
    return pl.pallas_call(
        functools.partial(_hgrn_prompt_kernel, layer=layer, tiles_per_seq=tps),
        grid=(m // tm,),
        in_specs=[pl.BlockSpec((tm, d), lambda i: (i, 0)), _const_spec((1, d)), _const_spec(w_in.shape),
                  _const_spec(lb_raw.shape), _const_spec((1, HEAD_DIM)), _const_spec(w_out.shape)],
        out_specs=[pl.BlockSpec((tm, d), lambda i: (i, 0)),
                   pl.BlockSpec((None, nh, HEAD_DIM, HEAD_DIM), lambda i: (i // tps, 0, 0, 0))],
        out_shape=[jax.ShapeDtypeStruct((m, d), F32),
                   jax.ShapeDtypeStruct((m // seq, nh, HEAD_DIM, HEAD_DIM), F32)],
        scratch_shapes=[pltpu.VMEM((nh, HEAD_DIM, HEAD_DIM), F32)] + [pltpu.VMEM((tm, d), F32)] * 5
        + [pltpu.VMEM((tm, d), BF16)],
        compiler_params=_params("arbitrary"),
    )(x, g.reshape(1, d), w_in, lb_raw, g_norm.reshape(1, HEAD_DIM), w_out)


def _head_proj_kernel(x_ref, g_ref, w_ref, out_ref):
    out_ref[...] = jnp.dot(_rms(x_ref[...], g_ref[...]).astype(BF16), w_ref[...], preferred_element_type=F32)


def head_proj(x, g, w4):
    m, d = x.shape
    ng, nh = w4.shape[:2]
    return pl.pallas_call(
        _head_proj_kernel,
        grid=(ng, nh),
        in_specs=[_const_spec((m, d)), _const_spec((1, d)),
                  pl.BlockSpec((None, None, d, HEAD_DIM), lambda j, h: (j, h, 0, 0))],
        out_specs=pl.BlockSpec((None, None, m, HEAD_DIM), lambda j, h: (j, h, 0, 0)),
        out_shape=jax.ShapeDtypeStruct((ng, nh, m, HEAD_DIM), F32),
        compiler_params=_params("arbitrary", "arbitrary"),
    )(x, g.reshape(1, d), w4)


def _lane_replicate(rows, eye, ones):
    lhs = []
    for r in rows:
        hi = jnp.where(eye, r, 0.0).astype(BF16)
        lo = jnp.where(eye, r - r.astype(BF16).astype(F32), 0.0).astype(BF16)
        lhs += [hi, lo]
    rep = jnp.dot(jnp.concatenate(lhs, axis=0), ones, preferred_element_type=F32)
    n = HEAD_DIM
    return [rep[2 * i * n:(2 * i + 1) * n] + rep[(2 * i + 1) * n:(2 * i + 2) * n] for i in range(len(rows))]


def _hgrn_decode_kernel(p_ref, s_ref, lb_ref, gn_ref, z_ref, sn_ref, *, layer):
    tb = s_ref.shape[0]
    lower = _lower_bound([lb_ref[pl.ds(j, 1), :] for j in range(lb_ref.shape[0])], layer)
    f_all, k_all = _hgrn_gates(p_ref[1], lower)
    q_all = _silu(p_ref[0])
    eye = lax.broadcasted_iota(jnp.int32, (HEAD_DIM, HEAD_DIM), 0) == lax.broadcasted_iota(
        jnp.int32, (HEAD_DIM, HEAD_DIM), 1)
    ones = jnp.ones((HEAD_DIM, HEAD_DIM), BF16)
    gn = gn_ref[...]
    for b in range(tb):
        sl = slice(b, b + 1)
        fr, kr, qr = _lane_replicate([f_all[sl], k_all[sl], q_all[sl]], eye, ones)
        s_new = fr * s_ref[b] + kr * p_ref[2, sl, :]
        sn_ref[b] = s_new
        o = jnp.sum(qr * s_new, axis=0, keepdims=True)
        z_ref[sl, :] = _rms(o, gn) * _silu(p_ref[3, sl, :])


def hgrn_decode(proj, state, lb_raw3, g_norm, *, layer):
    _, nh, bd, _ = proj.shape
    tb = 8
    assert bd % tb == 0
    depth = lb_raw3.shape[0]
    return pl.pallas_call(
        functools.partial(_hgrn_decode_kernel, layer=layer),
        grid=(nh, bd // tb),
        in_specs=[pl.BlockSpec((4, None, tb, HEAD_DIM), lambda h, t: (0, h, t, 0)),
                  pl.BlockSpec((tb, None, HEAD_DIM, HEAD_DIM), lambda h, t: (t, h, 0, 0)),
                  pl.BlockSpec((None, depth, HEAD_DIM), lambda h, t: (h, 0, 0)),
                  pl.BlockSpec((1, HEAD_DIM), lambda h, t: (0, 0))],
        out_specs=[pl.BlockSpec((None, tb, HEAD_DIM), lambda h, t: (h, t, 0)),
                   pl.BlockSpec((tb, None, HEAD_DIM, HEAD_DIM), lambda h, t: (t, h, 0, 0))],
        out_shape=[jax.ShapeDtypeStruct((nh, bd, HEAD_DIM), F32), jax.ShapeDtypeStruct(state.shape, F32)],
        compiler_params=_params("arbitrary", "arbitrary"),
    )(proj, state, lb_raw3.transpose(1, 0, 2), g_norm.reshape(1, HEAD_DIM))


def _rope_tables(pos):
    half = HEAD_DIM // 2
    inv = jnp.exp(-math.log(ROPE_THETA) * jnp.arange(half, dtype=F32) / half)
    ang = pos.astype(F32)[:, None] * inv[None, :]
    cos, sin = jnp.cos(ang), jnp.sin(ang)
    return jnp.concatenate([cos, cos], axis=-1), jnp.concatenate([-sin, sin], axis=-1)


def _qkv_kernel(x_ref, g_ref, w_ref, qn_ref, kn_ref, cos_ref, sin_ref, q_ref, k_ref, v_ref, km_ref):
    d = x_ref.shape[1]
    hb = _rms(x_ref[...], g_ref[...]).astype(BF16)
    cos, sin = cos_ref[...], sin_ref[...]

    def norm_rope(y, gain):
        y = _rms(y, gain)
        return y * cos + pltpu.roll(y, HEAD_DIM // 2, axis=1) * sin

    for hd in range(d // HEAD_DIM):
        sl = pl.ds(hd * HEAD_DIM, HEAD_DIM)
        q = jnp.dot(hb, w_ref[:, pl.ds(hd * HEAD_DIM, HEAD_DIM)], preferred_element_type=F32)
        k = jnp.dot(hb, w_ref[:, pl.ds(d + hd * HEAD_DIM, HEAD_DIM)], preferred_element_type=F32)
        q_ref[:, sl] = norm_rope(q, qn_ref[...])
        k = norm_rope(k, kn_ref[...])
        k_ref[:, sl] = k
        km_ref[:, sl] = jnp.mean(k, axis=0, keepdims=True)
        v_ref[:, sl] = jnp.dot(hb, w_ref[:, pl.ds(2 * d + hd * HEAD_DIM, HEAD_DIM)], preferred_element_type=F32)


def moba_qkv(x, g, w, qn, kn, cosf, sinf, *, tm):
    m, d = x.shape
    tps = cosf.shape[0] // tm
    return pl.pallas_call(
        _qkv_kernel,
        grid=(m // tm,),
        in_specs=[pl.BlockSpec((tm, d), lambda i: (i, 0)), _const_spec((1, d)), _const_spec(w.shape),
                  _const_spec((1, HEAD_DIM)), _const_spec((1, HEAD_DIM)),
                  pl.BlockSpec((tm, HEAD_DIM), lambda i: (i % tps, 0)),
                  pl.BlockSpec((tm, HEAD_DIM), lambda i: (i % tps, 0))],
        out_specs=[pl.BlockSpec((tm, d), lambda i: (i, 0))] * 3 + [pl.BlockSpec((None, 1, d), lambda i: (i, 0, 0))],
        out_shape=[jax.ShapeDtypeStruct((m, d), F32)] * 3 + [jax.ShapeDtypeStruct((m // tm, 1, d), F32)],
        compiler_params=_params("arbitrary"),
    )(x, g.reshape(1, d), w, qn.reshape(1, HEAD_DIM), kn.reshape(1, HEAD_DIM), cosf, sinf)


def _top_blocks(gate, n_idx, n_valid, axis):
    neg = -jnp.inf
    g = jnp.where(n_idx < n_valid, gate, neg)
    sel = jnp.zeros(gate.shape, F32)
    big = jnp.int32(2 ** 30)
    for _ in range(MB_TOPK):
        m = jnp.max(g, axis=axis, keepdims=True)
        idx = jnp.min(jnp.where(g == m, n_idx, big), axis=axis, keepdims=True)
        hit = n_idx == idx
        sel = jnp.where(hit, jnp.where(m > neg, 1.0, sel), sel)
        g = jnp.where(hit, neg, g)
    return sel


def _moba_prompt_kernel(q_ref, k_ref, v_ref, km_ref, o_ref, kb_ref, vt_ref, *, scale):
    nblk = km_ref.shape[0]
    blk = q_ref.shape[0] // nblk
    for n in range(nblk):
        rows = pl.ds(n * blk, blk)
        kb_ref[n] = k_ref[rows, :].astype(BF16)
        vt_ref[:, rows] = v_ref[rows, :].T.astype(BF16)
    km = km_ref[...]
    n_idx = lax.broadcasted_iota(jnp.int32, (nblk, blk), 0)
    causal = lax.broadcasted_iota(jnp.int32, (blk, blk), 0) <= lax.broadcasted_iota(jnp.int32, (blk, blk), 1)
    for i in range(nblk):
        q = q_ref[pl.ds(i * blk, blk), :]
        qb = (q * scale).astype(BF16)

        def scores(n):
            return lax.dot_general(kb_ref[n], qb, NT_DIMS, preferred_element_type=F32)

        s = [None] * i + [jnp.where(causal, scores(i), -jnp.inf)]
        if i > 0:
            gate = lax.dot_general(km, q, NT_DIMS, precision=lax.Precision.HIGHEST, preferred_element_type=F32)
            sel = _top_blocks(gate, n_idx, i, 0)
            for n in range(i):
                s[n] = jnp.where(sel[n:n + 1, :] > 0.5, scores(n), -jnp.inf)
        m = functools.reduce(jnp.maximum, [jnp.max(sn, axis=0, keepdims=True) for sn in s])
        p = [jnp.exp(sn - m) for sn in s]
        l = functools.reduce(jnp.add, [jnp.sum(pn, axis=0, keepdims=True) for pn in p])
        pcat = jnp.concatenate([pn.astype(BF16) for pn in p], axis=0)
        acc = jnp.dot(vt_ref[:, pl.ds(0, (i + 1) * blk)], pcat, preferred_element_type=F32)
        o_ref[pl.ds(i * blk, blk), :] = (acc / l).T


def moba_prompt_attention(q, k, v, kmeans, *, batch, seq):
    m, d = q.shape
    nh = d // HEAD_DIM
    nblk = seq // MB_BLOCK
    return pl.pallas_call(
        functools.partial(_moba_prompt_kernel, scale=HEAD_DIM ** -0.5),
        grid=(batch, nh),
        in_specs=[pl.BlockSpec((seq, HEAD_DIM), lambda b, h: (b, h))] * 3
        + [pl.BlockSpec((None, nblk, HEAD_DIM), lambda b, h: (b, 0, h))],
        out_specs=pl.BlockSpec((seq, HEAD_DIM), lambda b, h: (b, h)),
        out_shape=jax.ShapeDtypeStruct((m, d), F32),
        scratch_shapes=[pltpu.VMEM((nblk, MB_BLOCK, HEAD_DIM), BF16), pltpu.VMEM((HEAD_DIM, seq), BF16)],
        compiler_params=_params("arbitrary", "arbitrary"),
    )(q, k, v, kmeans)


def _moba_decode_kernel(pt_ref, q_ref, kn_ref, vn_ref, *rest, scale, n_pages, rows):
    del pt_ref
    k_refs, v_refs = rest[:n_pages], rest[n_pages:2 * n_pages]
    o_ref, s_ref, sel_ref, m_ref = rest[2 * n_pages:]
    nh = q_ref.shape[0]
    nblk = n_pages // 2
    phase = pl.program_id(1)
    page_rows = k_refs[0].shape[0]
    nchunk = page_rows // rows
    q = q_ref[...]
    lane = lax.broadcasted_iota(jnp.int32, (nh, HEAD_DIM), 1)

    def group_sum(a):
        return jnp.sum(a.reshape(rows // nh, nh, HEAD_DIM), axis=0)

    def group_max(a):
        return jnp.max(a.reshape(rows // nh, nh, HEAD_DIM), axis=0)

    def tile(a):
        return jnp.broadcast_to(a[None], (rows // nh, nh, HEAD_DIM)).reshape(rows, HEAD_DIM)

    @pl.when(phase == 0)
    def _():
        ones = jnp.ones((HEAD_DIM, HEAD_DIM), BF16)
        qs = tile(q * scale)
        gates = jnp.zeros((nh, HEAD_DIM), F32)
        mx = []
        for n in range(nblk):
            ksum = jnp.zeros((nh, HEAD_DIM), F32)
            mxn = jnp.full((nh, HEAD_DIM), -jnp.inf, F32)
            for e in range(2):
                ref = k_refs[2 * n + e]
                for c in range(nchunk):
                    kc = ref[pl.ds(c * rows, rows), :]
                    sc = jnp.dot((kc * qs).astype(BF16), ones, preferred_element_type=F32)
                    s_ref[n, pl.ds(e * page_rows + c * rows, rows), :] = sc
                    ksum = ksum + group_sum(kc)
                    mxn = jnp.maximum(mxn, group_max(sc))
            mx.append(mxn)
            mean = ksum / (2 * page_rows // nh)
            gates = jnp.where(lane == n, jnp.sum(q * mean, axis=1, keepdims=True), gates)
        sel = _top_blocks(gates, lane, nblk, 1)
        m = jnp.broadcast_to(jnp.sum(q * kn_ref[...], axis=1, keepdims=True) * scale, (nh, HEAD_DIM))
        for n in range(nblk):
            sn = jnp.broadcast_to(jnp.sum(jnp.where(lane == n, sel, 0.0), axis=1, keepdims=True), (nh, HEAD_DIM))
            sel_ref[n] = sn
            m = jnp.maximum(m, jnp.where(sn > 0.5, mx[n], -jnp.inf))
        m_ref[...] = m

    @pl.when(phase == 1)
    def _():
        m = m_ref[...]
        mt = tile(m)
        l = jnp.exp(jnp.sum(q * kn_ref[...], axis=1, keepdims=True) * scale - m)
        acc = l * vn_ref[...]
        for n in range(nblk):
            take = tile(sel_ref[n]) > 0.5
            for e in range(2):
                ref = v_refs[2 * n + e]
                for c in range(nchunk):
                    sc = s_ref[n, pl.ds(e * page_rows + c * rows, rows), :]
                    p = jnp.where(take, jnp.exp(sc - mt), 0.0)
                    l = l + group_sum(p)
                    acc = acc + group_sum(p * ref[pl.ds(c * rows, rows), :])
        o_ref[...] = acc / l


def moba_decode_attention(q, k_new, v_new, cache_k, cache_v, slot, page_table):
    bd, nh, _ = q.shape
    n_layers, n_phys, page = cache_k.shape[:3]
    n_pages = page_table.shape[1]
    assert 2 * page == MB_BLOCK and n_pages % 2 == 0
    nblk = n_pages // 2
    page_rows = page * nh
    ck = cache_k.reshape(n_layers, n_phys, page_rows, HEAD_DIM)
    cv = cache_v.reshape(n_layers, n_phys, page_rows, HEAD_DIM)

    def kmap(e):
        return lambda b, ph, pt: (slot, pt[jnp.minimum(b + ph, bd - 1), e], 0, 0)

    def vmap_(e):
        return lambda b, ph, pt: (slot, pt[b, e], 0, 0)

    page_block = (None, None, page_rows, HEAD_DIM)
    vec = pl.BlockSpec((None, nh, HEAD_DIM), lambda b, ph, pt: (b, 0, 0))
    grid_spec = pltpu.PrefetchScalarGridSpec(
        num_scalar_prefetch=1,
        grid=(bd, 2),
        in_specs=[vec, vec, vec] + [pl.BlockSpec(page_block, kmap(e)) for e in range(n_pages)]
        + [pl.BlockSpec(page_block, vmap_(e)) for e in range(n_pages)],
        out_specs=vec,
        scratch_shapes=[pltpu.VMEM((nblk, 2 * page_rows, HEAD_DIM), F32), pltpu.VMEM((nblk, nh, HEAD_DIM), F32),
                        pltpu.VMEM((nh, HEAD_DIM), F32)],
    )
    return pl.pallas_call(
        functools.partial(_moba_decode_kernel, scale=HEAD_DIM ** -0.5, n_pages=n_pages, rows=256),
        grid_spec=grid_spec,
        out_shape=jax.ShapeDtypeStruct((bd, nh, HEAD_DIM), F32),
        compiler_params=_params("arbitrary", "arbitrary"),
    )(page_table, q, k_new, v_new, *([ck] * n_pages), *([cv] * n_pages))


def kernel(x_prompt, x_sample, state_shortconv, state_hgrn, cache_k, cache_v, page_table, state_ffn_conv, norm_mix, norm_ffn, w_in_a, w_conv_a, w_out_a, w_in_b, lb_raw, g_norm_b, w_out_b, w_qkv_c, q_norm_c, k_norm_c, w_out_c, w_up, w_ffn_conv, b_ffn_conv, w_down):
    bp, seq, d = x_prompt.shape
    bd = x_sample.shape[0]
    assert x_sample.shape[1] == 1
    depth = norm_mix.shape[0]
    nh = d // HEAD_DIM
    past_len = page_table.shape[1] * cache_k.shape[2]
    assert past_len % MB_BLOCK == 0 and seq % MB_BLOCK == 0

    xp = x_prompt.reshape(bp * seq, d)
    xs = x_sample.reshape(bd, d)
    lb_raw3 = lb_raw.reshape(depth, nh, HEAD_DIM)
    outs = {name: [] for name in ("sc_p", "sc_s", "hg_p", "hg_s", "kp", "vp", "ks", "vs", "fc_p", "fc_s")}
    n_conv = n_hgrn = n_moba = 0
    for i in range(depth):
        mixer = i % 3
        if mixer == 0:
            j, n_conv = n_conv, n_conv + 1
            w1, w2 = w_in_a[j].astype(BF16), w_out_a[j].astype(BF16)
            xp, st_p = gated_conv_block_prompt(xp, norm_mix[i], w1, w_conv_a[j], None, w2, seq=seq)
            xs, st_s = gated_conv_block_decode(xs, norm_mix[i], w1, w_conv_a[j], None, w2, state_shortconv[j])
            outs["sc_p"].append(st_p)
            outs["sc_s"].append(st_s)
        elif mixer == 1:
            j, n_hgrn = n_hgrn, n_hgrn + 1
            wi, wo = w_in_b[j].astype(BF16), w_out_b[j].astype(BF16)
            w4 = wi.reshape(d, 4, nh, HEAD_DIM).transpose(1, 2, 0, 3)
            wo3 = wo.reshape(nh, HEAD_DIM, d)
            xp, st_p = hgrn_prompt(xp, norm_mix[i], wi, lb_raw, g_norm_b[j], wo, layer=i, seq=seq)
            proj = head_proj(xs, norm_mix[i], w4)
            z, st_s = hgrn_decode(proj, state_hgrn[j], lb_raw3, g_norm_b[j], layer=i)
            xs = proj_residual_heads(z, wo3, xs)
            outs["hg_p"].append(st_p)
            outs["hg_s"].append(st_s)
        else:
            j, n_moba = n_moba, n_moba + 1
            wq, wo = w_qkv_c[j].astype(BF16), w_out_c[j].astype(BF16)
            cos_p, sin_p = _rope_tables(jnp.arange(seq, dtype=jnp.int32))
            q, k, v, km = moba_qkv(xp, norm_mix[i], wq, q_norm_c[j], k_norm_c[j], cos_p, sin_p, tm=MB_BLOCK)
            o = moba_prompt_attention(q, k, v, km.reshape(bp, seq // MB_BLOCK, d), batch=bp, seq=seq)
            xp = proj_residual(o, wo, xp)
            cos_s, sin_s = _rope_tables(jnp.full((bd,), past_len, jnp.int32))
            qs, ks, vs, _ = moba_qkv(xs, norm_mix[i], wq, q_norm_c[j], k_norm_c[j], cos_s, sin_s, tm=bd)
            os_ = moba_decode_attention(qs.reshape(bd, nh, HEAD_DIM), ks.reshape(bd, nh, HEAD_DIM),
                                        vs.reshape(bd, nh, HEAD_DIM), cache_k, cache_v, j, page_table)
            xs = proj_residual(os_.reshape(bd, d), wo, xs)
            outs["kp"].append(k.reshape(bp, seq, nh, HEAD_DIM))
            outs["vp"].append(v.reshape(bp, seq, nh, HEAD_DIM))
            outs["ks"].append(ks.reshape(bd, 1, nh, HEAD_DIM))
            outs["vs"].append(vs.reshape(bd, 1, nh, HEAD_DIM))
        w1, w2 = w_up[i].astype(BF16), w_down[i].astype(BF16)
        xp, st_p = gated_conv_block_prompt(xp, norm_ffn[i], w1, w_ffn_conv[i], b_ffn_conv[i], w2, seq=seq)
        xs, st_s = gated_conv_block_decode(xs, norm_ffn[i], w1, w_ffn_conv[i], b_ffn_conv[i], w2, state_ffn_conv[i])
        outs["fc_p"].append(st_p)
        outs["fc_s"].append(st_s)
    return (xp.reshape(bp, seq, d), xs.reshape(bd, 1, d), jnp.stack(outs["sc_p"]), jnp.stack(outs["sc_s"]),
            jnp.stack(outs["hg_p"]), jnp.stack(outs["hg_s"]), jnp.stack(outs["kp"]), jnp.stack(outs["vp"]),
            jnp.stack(outs["ks"]), jnp.stack(outs["vs"]), jnp.stack(outs["fc_p"]), jnp.stack(outs["fc_s"]))
```

```python
import functools
import math

import jax
import jax.numpy as jnp
from jax import lax
from jax.experimental import pallas as pl
from jax.experimental.pallas import tpu as pltpu

F32 = jnp.float32
BF16 = jnp.bfloat16

EPS = 1e-6
HEAD_DIM = 128
CONV_TAPS = 3
HG_CHUNK = 64
HG_SUB = 16
MB_BLOCK = 256
MB_TOPK = 3
ROPE_THETA = 10000.0
V7X_VMEM_BYTES = 64 * 1024 * 1024
VMEM_LIMIT = V7X_VMEM_BYTES - 8 * 1024 * 1024

NT_DIMS = (((1,), (1,)), ((), ()))
TN_DIMS = (((0,), (0,)), ((), ()))


def _params(*sem):
    return pltpu.CompilerParams(dimension_semantics=sem, vmem_limit_bytes=VMEM_LIMIT)


def _const_spec(shape):
    n = len(shape)
    return pl.BlockSpec(shape, lambda *_: (0,) * n, pipeline_mode=pl.Buffered(1))


def _layer_spec(stack, layer):
    shape = stack.shape[1:]
    return pl.BlockSpec((None,) + shape, lambda *_: (layer,) + (0,) * len(shape), pipeline_mode=pl.Buffered(1))


def _rms(x, g):
    return x * lax.rsqrt(jnp.mean(x * x, axis=-1, keepdims=True) + EPS) * g


def _sigmoid(x):
    return 1.0 / (1.0 + jnp.exp(-x))


def _silu(x):
    return x * _sigmoid(x)


def _bdot(a, b):
    return jnp.dot(a.astype(BF16), b.astype(BF16), preferred_element_type=F32)


def _bdot_nt(a, b):
    return lax.dot_general(a.astype(BF16), b.astype(BF16), NT_DIMS, preferred_element_type=F32)


def _gcb_kernel(*refs, ffn, decode, pre, tiles_per_seq, width, cw):
    refs = list(refs)
    o_ref, wo_ref = (refs.pop(0), refs.pop(0)) if pre else (None, None)
    x_ref, g_ref, w1_ref, wc_ref = refs[:4]
    refs = refs[4:]
    bc_ref = refs.pop(0) if ffn else None
    w2_ref = refs.pop(0)
    if decode:
        sm2_ref, sm1_ref, out_ref, unew_ref, z_ref = refs
    else:
        out_ref, st_ref, z_ref, carry_ref = refs
    x = x_ref[...]
    if pre:
        x = x + _bdot(o_ref[...], wo_ref[...])
    tm = x.shape[0]
    h = _rms(x, g_ref[...]).astype(BF16)
    if not decode:
        @pl.when(pl.program_id(0) % tiles_per_seq == 0)
        def _():
            carry_ref[...] = jnp.zeros_like(carry_ref)
        row = lax.broadcasted_iota(jnp.int32, (tm, cw), 0)
    for c in range(width // cw):
        sl = pl.ds(c * cw, cw)

        def proj(j):
            return jnp.dot(h, w1_ref[:, pl.ds(j * width + c * cw, cw)], preferred_element_type=F32)

        if ffn:
            u, gate = proj(0), proj(1)
        else:
            bg = proj(0)
            u = proj(1) * proj(2)
        if decode:
            um2, um1 = sm2_ref[:, sl], sm1_ref[:, sl]
            unew_ref[:, sl] = u
        else:
            c0, c1 = carry_ref[0:1, sl], carry_ref[1:2, sl]
            um1 = jnp.where(row == 0, c1, pltpu.roll(u, 1, axis=0))
            um2 = jnp.where(row == 0, c0, jnp.where(row == 1, c1, pltpu.roll(u, 2, axis=0)))
            carry_ref[0:2, sl] = u[tm - 2:tm, :]
        wc = wc_ref[:, sl]
        conv = wc[0:1] * um2 + wc[1:2] * um1 + wc[2:3] * u
        z = _silu(conv + bc_ref[:, sl]) * gate if ffn else bg * conv
        z_ref[:, sl] = z.astype(BF16)
    out_ref[...] = x + jnp.dot(z_ref[...], w2_ref[...], preferred_element_type=F32)
    if not decode:
        st_ref[...] = carry_ref[0:2, :]


def _gcb_tile(rows):
    for tm in (512, 256, 128, 64, 32, 16, 8):
        if rows % tm == 0:
            return tm
    raise ValueError(f"row count {rows} must be a multiple of 8")


def gated_conv_block_prompt(x, g, w1, wc, bc, w2, *, seq, pre=None):
    m, d = x.shape
    width = w2[0].shape[1]
    ffn = bc is not None
    tm = _gcb_tile(seq)
    tps = seq // tm
    cw = 256
    args = [x, g.reshape(1, d), w1[0], wc] + ([bc.reshape(1, width)] if ffn else []) + [w2[0]]
    in_specs = [pl.BlockSpec((tm, d), lambda i: (i, 0)), _const_spec((1, d)), _layer_spec(*w1),
                _const_spec(wc.shape)] + ([_const_spec((1, width))] if ffn else []) + [_layer_spec(*w2)]
    if pre:
        args = [pre[0], pre[1][0]] + args
        in_specs = [pl.BlockSpec((tm, pre[0].shape[1]), lambda i: (i, 0)), _layer_spec(*pre[1])] + in_specs
    return pl.pallas_call(
        functools.partial(_gcb_kernel, ffn=ffn, decode=False, pre=bool(pre), tiles_per_seq=tps, width=width, cw=cw),
        grid=(m // tm,),
        in_specs=in_specs,
        out_specs=[pl.BlockSpec((tm, d), lambda i: (i, 0)),
                   pl.BlockSpec((None, CONV_TAPS - 1, width), lambda i: (i // tps, 0, 0))],
        out_shape=[jax.ShapeDtypeStruct((m, d), F32),
                   jax.ShapeDtypeStruct((m // seq, CONV_TAPS - 1, width), F32)],
        scratch_shapes=[pltpu.VMEM((tm, width), BF16), pltpu.VMEM((8, width), F32)],
        compiler_params=_params("arbitrary"),
    )(*args)


def gated_conv_block_decode(x, g, w1, wc, bc, w2, state, pre=None):
    m, d = x.shape
    width = w2[0].shape[1]
    ffn = bc is not None
    cw = 256
    args = [x, g.reshape(1, d), w1[0], wc] + ([bc.reshape(1, width)] if ffn else []) + [
        w2[0], state[:, 0], state[:, 1]]
    in_specs = [_const_spec(a.shape) for a in args]
    in_specs[2] = _layer_spec(*w1)
    in_specs[-3] = _layer_spec(*w2)
    if pre:
        args = [pre[0], pre[1][0]] + args
        in_specs = [_const_spec(pre[0].shape), _layer_spec(*pre[1])] + in_specs
    out, unew = pl.pallas_call(
        functools.partial(_gcb_kernel, ffn=ffn, decode=True, pre=bool(pre), tiles_per_seq=1, width=width, cw=cw),
        grid=(1,),
        in_specs=in_specs,
        out_specs=[pl.BlockSpec((m, d), lambda i: (0, 0)), pl.BlockSpec((m, width), lambda i: (0, 0))],
        out_shape=[jax.ShapeDtypeStruct((m, d), F32), jax.ShapeDtypeStruct((m, width), F32)],
        scratch_shapes=[pltpu.VMEM((m, width), BF16)],
        compiler_params=_params("arbitrary"),
    )(*args)
    return out, jnp.stack([state[:, 1], unew], axis=1)


def _proj_residual_kernel(z_ref, w_ref, x_ref, out_ref):
    acc = x_ref[...]
    for hd in range(z_ref.shape[0]):
        acc = acc + _bdot(z_ref[hd], w_ref[hd])
    out_ref[...] = acc


def proj_residual_heads(z, w, x):
    nh, m, hd = z.shape
    d = x.shape[1]
    tm = _gcb_tile(m)
    return pl.pallas_call(
        _proj_residual_kernel,
        grid=(m // tm,),
        in_specs=[pl.BlockSpec((nh, tm, hd), lambda i: (0, i, 0)), _layer_spec(*w),
                  pl.BlockSpec((tm, d), lambda i: (i, 0))],
        out_specs=pl.BlockSpec((tm, d), lambda i: (i, 0)),
        out_shape=jax.ShapeDtypeStruct((m, d), F32),
        compiler_params=_params("arbitrary"),
    )(z, w[0], x)


def _lower_bound(rows, layer):
    mx = functools.reduce(jnp.maximum, rows)
    e = [jnp.exp(r - mx) for r in rows]
    tot = functools.reduce(jnp.add, e)
    w = [a / tot for a in e]
    cum = w[0]
    for j in range(1, layer + 1):
        cum = cum + w[j]
    return cum - w[0]


def _split3(x):
    hi = x.astype(BF16)
    r = x - hi.astype(F32)
    mid = r.astype(BF16)
    lo = (r - mid.astype(F32)).astype(BF16)
    return hi, mid, lo


def _level_ref(cum, b):
    t = cum.shape[0]
    if b >= 4:
        c3 = cum.reshape(t // (2 * b), 2 * b, HEAD_DIM)
        return jnp.broadcast_to(c3[:, b - 1:b, :], c3.shape).reshape(t, HEAD_DIM)
    r = lax.broadcasted_iota(jnp.int32, cum.shape, 0) & (2 * b - 1)
    prev = pltpu.roll(cum, 1, axis=0)
    if b == 1:
        return jnp.where(r == 1, prev, cum)
    return jnp.where(r == 0, pltpu.roll(cum, t - 1, axis=0),
                     jnp.where(r == 1, cum, jnp.where(r == 2, prev, pltpu.roll(cum, 2, axis=0))))


def _hgrn_tile(q, k, v, cum, st, xor_ts, below):
    t = q.shape[0]
    o = _bdot_nt(q * jnp.exp(cum), st) + jnp.sum(q * k, axis=-1, keepdims=True) * v
    a = None
    b = t // 2
    while b >= 1:
        x = jnp.exp(-jnp.abs(cum - _level_ref(cum, b)))
        al = _bdot_nt(q * x, k * x)
        a = al if a is None else jnp.where(xor_ts < 2 * b, al, a)
        b //= 2
    o = o + _bdot(jnp.where(below, a, 0.0), v)
    last = cum[t - 1:t]
    upd = lax.dot_general(v.astype(BF16), (k * jnp.exp(last - cum)).astype(BF16), TN_DIMS,
                          preferred_element_type=F32)
    return o, st * jnp.exp(last) + upd


def _hgrn_gates(fz, lower):
    f = lower + (1.0 - lower) * _sigmoid(fz)
    k = (1.0 - lower) * _sigmoid(-fz)
    return f, k


def _hgrn_prompt_kernel(x_ref, g_ref, w_ref, lb_ref, gn_ref, wo_ref, out_ref, st_ref,
                        s_ref, q_s, k_s, v_s, gg_s, cum_s, z_s, *, layer, tiles_per_seq):
    tm, d = x_ref.shape
    nh = d // HEAD_DIM
    t = pl.program_id(0)

    @pl.when(t % tiles_per_seq == 0)
    def _():
        s_ref[...] = jnp.zeros_like(s_ref)

    def proj(j):
        return jnp.dot(hb, w_ref[:, pl.ds(j * d, d)], preferred_element_type=F32)

    x = x_ref[...]
    hb = _rms(x, g_ref[...]).astype(BF16)
    lower = _lower_bound([lb_ref[j:j + 1, :] for j in range(lb_ref.shape[0])], layer)
    f, kk = _hgrn_gates(proj(1), lower)
    k_s[...] = kk
    row = lax.broadcasted_iota(jnp.int32, (tm, tm), 0)
    col = lax.broadcasted_iota(jnp.int32, (tm, tm), 1)
    tri = jnp.where(row >= col, 1.0, 0.0).astype(BF16)
    cum_s[...] = functools.reduce(jnp.add, [jnp.dot(tri, part, preferred_element_type=F32)
                                            for part in _split3(jnp.log(f))])
    q_s[...] = _silu(proj(0))
    v_s[...] = proj(2)
    gg_s[...] = proj(3)
    xor_ts = row ^ col
    below = row > col
    gn = gn_ref[...]
    for hd in range(nh):
        sl = pl.ds(hd * HEAD_DIM, HEAD_DIM)
        o, st_new = _hgrn_tile(q_s[:, sl], k_s[:, sl], v_s[:, sl], cum_s[:, sl], s_ref[hd], xor_ts, below)
        s_ref[hd] = st_new
        z_s[:, sl] = (_rms(o, gn) * _silu(gg_s[:, sl])).astype(BF16)
    out_ref[...] = x + jnp.dot(z_s[...], wo_ref[...], preferred_element_type=F32)

    @pl.when(t % tiles_per_seq == tiles_per_seq - 1)
    def _():
        for hd in range(nh):
            st_ref[hd] = s_ref[hd].T


def hgrn_prompt(x, g, w_in, lb_raw, g_norm, w_out, *, layer, seq):
    m, d = x.shape
    nh = d // HEAD_DIM
    tm = min(_gcb_tile(seq), 256)
    tps = seq // tm
    return pl.pallas_call(
        functools.partial(_hgrn_prompt_kernel, layer=layer, tiles_per_seq=tps),
        grid=(m // tm,),
        in_specs=[pl.BlockSpec((tm, d), lambda i: (i, 0)), _const_spec((1, d)), _layer_spec(*w_in),
                  _const_spec(lb_raw.shape), _const_spec((1, HEAD_DIM)), _layer_spec(*w_out)],
        out_specs=[pl.BlockSpec((tm, d), lambda i: (i, 0)),
                   pl.BlockSpec((None, nh, HEAD_DIM, HEAD_DIM), lambda i: (i // tps, 0, 0, 0))],
        out_shape=[jax.ShapeDtypeStruct((m, d), F32),
                   jax.ShapeDtypeStruct((m // seq, nh, HEAD_DIM, HEAD_DIM), F32)],
        scratch_shapes=[pltpu.VMEM((nh, HEAD_DIM, HEAD_DIM), F32)] + [pltpu.VMEM((tm, d), F32)] * 5
        + [pltpu.VMEM((tm, d), BF16)],
        compiler_params=_params("arbitrary"),
    )(x, g.reshape(1, d), w_in[0], lb_raw, g_norm.reshape(1, HEAD_DIM), w_out[0])


def _head_proj_kernel(x_ref, g_ref, w_ref, out_ref):
    out_ref[...] = jnp.dot(_rms(x_ref[...], g_ref[...]).astype(BF16), w_ref[...], preferred_element_type=F32)


def head_proj(x, g, w5, layer):
    m, d = x.shape
    ng, nh = w5.shape[1:3]
    return pl.pallas_call(
        _head_proj_kernel,
        grid=(ng, nh),
        in_specs=[_const_spec((m, d)), _const_spec((1, d)),
                  pl.BlockSpec((None, None, None, d, HEAD_DIM), lambda j, h: (layer, j, h, 0, 0))],
        out_specs=pl.BlockSpec((None, None, m, HEAD_DIM), lambda j, h: (j, h, 0, 0)),
        out_shape=jax.ShapeDtypeStruct((ng, nh, m, HEAD_DIM), F32),
        compiler_params=_params("arbitrary", "arbitrary"),
    )(x, g.reshape(1, d), w5)


def _lane_replicate(rows, eye, ones):
    lhs = [jnp.where(eye, r, 0.0).astype(BF16) for r in rows]
    lhs.append(jnp.where(eye, rows[0] - rows[0].astype(BF16).astype(F32), 0.0).astype(BF16))
    rep = jnp.dot(jnp.concatenate(lhs, axis=0), ones, preferred_element_type=F32)
    n = HEAD_DIM
    out = [rep[i * n:(i + 1) * n] for i in range(len(rows))]
    out[0] = out[0] + rep[len(rows) * n:(len(rows) + 1) * n]
    return out


def _hgrn_decode_kernel(p_ref, s_ref, lb_ref, gn_ref, z_ref, sn_ref, *, layer):
    tb = s_ref.shape[0]
    lower = _lower_bound([lb_ref[pl.ds(j, 1), :] for j in range(lb_ref.shape[0])], layer)
    f_all, k_all = _hgrn_gates(p_ref[1], lower)
    q_all = _silu(p_ref[0])
    eye = lax.broadcasted_iota(jnp.int32, (HEAD_DIM, HEAD_DIM), 0) == lax.broadcasted_iota(
        jnp.int32, (HEAD_DIM, HEAD_DIM), 1)
    ones = jnp.ones((HEAD_DIM, HEAD_DIM), BF16)
    gn = gn_ref[...]
    for b in range(tb):
        sl = slice(b, b + 1)
        fr, kr, qr = _lane_replicate([f_all[sl], k_all[sl], q_all[sl]], eye, ones)
        s_new = fr * s_ref[b] + kr * p_ref[2, sl, :]
        sn_ref[b] = s_new
        o = jnp.sum(qr * s_new, axis=0, keepdims=True)
        z_ref[sl, :] = _rms(o, gn) * _silu(p_ref[3, sl, :])


def hgrn_decode(proj, states, slot, lb_raw3, g_norm, *, layer):
    _, nh, bd, _ = proj.shape
    tb = 8
    assert bd % tb == 0
    depth = lb_raw3.shape[0]
    return pl.pallas_call(
        functools.partial(_hgrn_decode_kernel, layer=layer),
        grid=(nh, bd // tb),
        in_specs=[pl.BlockSpec((4, None, tb, HEAD_DIM), lambda h, t: (0, h, t, 0)),
                  pl.BlockSpec((None, tb, None, HEAD_DIM, HEAD_DIM), lambda h, t: (slot, t, h, 0, 0)),
                  pl.BlockSpec((None, depth, HEAD_DIM), lambda h, t: (h, 0, 0)),
                  pl.BlockSpec((1, HEAD_DIM), lambda h, t: (0, 0))],
        out_specs=[pl.BlockSpec((None, tb, HEAD_DIM), lambda h, t: (h, t, 0)),
                   pl.BlockSpec((tb, None, HEAD_DIM, HEAD_DIM), lambda h, t: (t, h, 0, 0))],
        out_shape=[jax.ShapeDtypeStruct((nh, bd, HEAD_DIM), F32), jax.ShapeDtypeStruct(states.shape[1:], F32)],
        compiler_params=_params("arbitrary", "arbitrary"),
    )(proj, states, lb_raw3.transpose(1, 0, 2), g_norm.reshape(1, HEAD_DIM))


def _rope_tables(pos):
    half = HEAD_DIM // 2
    inv = jnp.exp(-math.log(ROPE_THETA) * jnp.arange(half, dtype=F32) / half)
    ang = pos.astype(F32)[:, None] * inv[None, :]
    cos, sin = jnp.cos(ang), jnp.sin(ang)
    return jnp.concatenate([cos, cos], axis=-1), jnp.concatenate([-sin, sin], axis=-1)


def _qkv_kernel(x_ref, g_ref, w_ref, qn_ref, kn_ref, cos_ref, sin_ref, q_ref, k_ref, v_ref, km_ref):
    d = x_ref.shape[1]
    hb = _rms(x_ref[...], g_ref[...]).astype(BF16)
    cos, sin = cos_ref[...], sin_ref[...]

    def norm_rope(y, gain):
        y = _rms(y, gain)
        return y * cos + pltpu.roll(y, HEAD_DIM // 2, axis=1) * sin

    v_ref[...] = jnp.dot(hb, w_ref[:, pl.ds(2 * d, d)], preferred_element_type=F32)
    for j, (out_ref, gain_ref) in enumerate(((q_ref, qn_ref), (k_ref, kn_ref))):
        y = jnp.dot(hb, w_ref[:, pl.ds(j * d, d)], preferred_element_type=F32)
        for hd in range(d // HEAD_DIM):
            sl = pl.ds(hd * HEAD_DIM, HEAD_DIM)
            yh = norm_rope(y[:, hd * HEAD_DIM:(hd + 1) * HEAD_DIM], gain_ref[...])
            out_ref[:, sl] = yh
            if j == 1:
                km_ref[:, sl] = jnp.mean(yh, axis=0, keepdims=True)


def moba_qkv(x, g, w, qn, kn, cosf, sinf, *, tm):
    m, d = x.shape
    tps = cosf.shape[0] // tm
    return pl.pallas_call(
        _qkv_kernel,
        grid=(m // tm,),
        in_specs=[pl.BlockSpec((tm, d), lambda i: (i, 0)), _const_spec((1, d)), _layer_spec(*w),
                  _const_spec((1, HEAD_DIM)), _const_spec((1, HEAD_DIM)),
                  pl.BlockSpec((tm, HEAD_DIM), lambda i: (i % tps, 0)),
                  pl.BlockSpec((tm, HEAD_DIM), lambda i: (i % tps, 0))],
        out_specs=[pl.BlockSpec((tm, d), lambda i: (i, 0))] * 3 + [pl.BlockSpec((None, 1, d), lambda i: (i, 0, 0))],
        out_shape=[jax.ShapeDtypeStruct((m, d), F32)] * 3 + [jax.ShapeDtypeStruct((m // tm, 1, d), F32)],
        compiler_params=_params("arbitrary"),
    )(x, g.reshape(1, d), w[0], qn.reshape(1, HEAD_DIM), kn.reshape(1, HEAD_DIM), cosf, sinf)


def _top_blocks(gate, n_idx, n_valid, axis):
    neg = -jnp.inf
    g = jnp.where(n_idx < n_valid, gate, neg)
    sel = jnp.zeros(gate.shape, F32)
    big = jnp.int32(2 ** 30)
    for _ in range(MB_TOPK):
        m = jnp.max(g, axis=axis, keepdims=True)
        idx = jnp.min(jnp.where(g == m, n_idx, big), axis=axis, keepdims=True)
        hit = n_idx == idx
        sel = jnp.where(hit, jnp.where(m > neg, 1.0, sel), sel)
        g = jnp.where(hit, neg, g)
    return sel


def _moba_prompt_kernel(q_ref, k_ref, v_ref, km_ref, o_ref, kb_ref, vt_ref, *, scale):
    nblk = km_ref.shape[0]
    blk = q_ref.shape[0] // nblk
    for n in range(nblk):
        rows = pl.ds(n * blk, blk)
        kb_ref[n] = k_ref[rows, :].astype(BF16)
        vt_ref[:, rows] = v_ref[rows, :].T.astype(BF16)
    km = km_ref[...]
    n_idx = lax.broadcasted_iota(jnp.int32, (nblk, blk), 0)
    causal = lax.broadcasted_iota(jnp.int32, (blk, blk), 0) <= lax.broadcasted_iota(jnp.int32, (blk, blk), 1)
    for i in range(nblk):
        q = q_ref[pl.ds(i * blk, blk), :]
        qb = (q * scale).astype(BF16)

        def scores(n):
            return lax.dot_general(kb_ref[n], qb, NT_DIMS, preferred_element_type=F32)

        s = [None] * i + [jnp.where(causal, scores(i), -jnp.inf)]
        if i > 0:
            gate = lax.dot_general(km, q, NT_DIMS, precision=lax.Precision.HIGHEST, preferred_element_type=F32)
            sel = _top_blocks(gate, n_idx, i, 0)
            for n in range(i):
                s[n] = jnp.where(sel[n:n + 1, :] > 0.5, scores(n), -jnp.inf)
        m = functools.reduce(jnp.maximum, [jnp.max(sn, axis=0, keepdims=True) for sn in s])
        p = [jnp.exp(sn - m) for sn in s]
        l = functools.reduce(jnp.add, [jnp.sum(pn, axis=0, keepdims=True) for pn in p])
        pcat = jnp.concatenate([pn.astype(BF16) for pn in p], axis=0)
        acc = jnp.dot(vt_ref[:, pl.ds(0, (i + 1) * blk)], pcat, preferred_element_type=F32)
        o_ref[pl.ds(i * blk, blk), :] = (acc / l).T


def moba_prompt_attention(q, k, v, kmeans, *, batch, seq):
    m, d = q.shape
    nh = d // HEAD_DIM
    nblk = seq // MB_BLOCK
    return pl.pallas_call(
        functools.partial(_moba_prompt_kernel, scale=HEAD_DIM ** -0.5),
        grid=(batch, nh),
        in_specs=[pl.BlockSpec((seq, HEAD_DIM), lambda b, h: (b, h))] * 3
        + [pl.BlockSpec((None, nblk, HEAD_DIM), lambda b, h: (b, 0, h))],
        out_specs=pl.BlockSpec((seq, HEAD_DIM), lambda b, h: (b, h)),
        out_shape=jax.ShapeDtypeStruct((m, d), F32),
        scratch_shapes=[pltpu.VMEM((nblk, MB_BLOCK, HEAD_DIM), BF16), pltpu.VMEM((HEAD_DIM, seq), BF16)],
        compiler_params=_params("arbitrary", "arbitrary"),
    )(q, k, v, kmeans)


def _top_block_ids(gate, n_idx, n_take):
    g = gate
    ids = []
    for _ in range(n_take):
        m = jnp.max(g, axis=1, keepdims=True)
        idx = jnp.min(jnp.where(g == m, n_idx, jnp.int32(2 ** 30)), axis=1, keepdims=True)
        ids.append(idx)
        g = jnp.where(n_idx == idx, -jnp.inf, g)
    return ids


def _moba_decode_kernel(pt_ref, q_ref, kn_ref, vn_ref, *rest, scale, n_pages, slot, n_samples):
    k_refs = rest[:n_pages]
    cv_hbm, o_ref, s_ref, vbuf, stat_ref, ids_ref, sems = rest[n_pages:]
    nh = q_ref.shape[0]
    nblk = n_pages // 2
    nsel = min(MB_TOPK, nblk)
    page = k_refs[0].shape[0] // nh
    t = pl.program_id(0)
    par = t % 2

    def v_copy(p, b, h, r, e):
        n = ids_ref[p, h * nsel + r]
        return pltpu.make_async_copy(cv_hbm.at[slot, pt_ref[b, 2 * n + e], :, h, :],
                                     vbuf.at[p, h, r, pl.ds(e * page, page), :], sems.at[p])

    def v_copies(p, b):
        return [v_copy(p, b, h, r, e) for h in range(nh) for r in range(nsel) for e in range(2)]

    @pl.when(t < n_samples)
    def _():
        q = q_ref[...]
        lane = lax.broadcasted_iota(jnp.int32, (nh, HEAD_DIM), 1)
        ones = jnp.ones((HEAD_DIM, HEAD_DIM), BF16)
        gates = jnp.full((nh, HEAD_DIM), -jnp.inf, F32)
        mx = []
        for n in range(nblk):
            ksum, mxn = [], []
            for h in range(nh):
                qh = q[h:h + 1, :] * scale
                ks = jnp.zeros((1, HEAD_DIM), F32)
                mh = jnp.full((1, HEAD_DIM), -jnp.inf, F32)
                for e in range(2):
                    kh = k_refs[2 * n + e][pl.ds(h, page, stride=nh), :]
                    sc = jnp.dot((kh * qh).astype(BF16), ones, preferred_element_type=F32)
                    s_ref[par, n, h, pl.ds(e * page, page), :] = sc
                    ks = ks + jnp.sum(kh, axis=0, keepdims=True)
                    mh = jnp.maximum(mh, jnp.max(sc, axis=0, keepdims=True))
                ksum.append(ks)
                mxn.append(mh)
            mx.append(jnp.concatenate(mxn, axis=0))
            mean = jnp.concatenate(ksum, axis=0) / (2 * page)
            gates = jnp.where(lane == n, jnp.sum(q * mean, axis=1, keepdims=True), gates)
        ids = _top_block_ids(gates, lane, nsel)
        s_own = jnp.sum(q * kn_ref[...], axis=1, keepdims=True) * scale
        m = jnp.broadcast_to(s_own, (nh, HEAD_DIM))
        ids_b = [jnp.broadcast_to(idx, (nh, HEAD_DIM)) for idx in ids]
        for n in range(nblk):
            mx_n = -jnp.inf
            for idx in ids_b:
                mx_n = jnp.where(idx == n, mx[n], mx_n)
            m = jnp.maximum(m, mx_n)
        p_own = jnp.exp(s_own - m)
        stat_ref[par, 0] = m
        stat_ref[par, 1] = p_own
        stat_ref[par, 2] = p_own * vn_ref[...]
        for r in range(nsel):
            for h in range(nh):
                ids_ref[par, h * nsel + r] = ids[r][h, 0]
        for cp in v_copies(par, t):
            cp.start()

    @pl.when(t >= 1)
    def _():
        prev = 1 - par
        for cp in v_copies(prev, t - 1):
            cp.wait()
        m = stat_ref[prev, 0]
        l_rows, acc_rows = [], []
        for h in range(nh):
            l = jnp.zeros((1, HEAD_DIM), F32)
            acc = jnp.zeros((1, HEAD_DIM), F32)
            for r in range(nsel):
                p = jnp.exp(s_ref[prev, ids_ref[prev, h * nsel + r], h] - m[h:h + 1, :])
                l = l + jnp.sum(p, axis=0, keepdims=True)
                acc = acc + jnp.sum(p * vbuf[prev, h, r], axis=0, keepdims=True)
            l_rows.append(l)
            acc_rows.append(acc)
        l = stat_ref[prev, 1] + jnp.concatenate(l_rows, axis=0)
        o_ref[...] = (stat_ref[prev, 2] + jnp.concatenate(acc_rows, axis=0)) / l


def moba_decode_attention(q, k_new, v_new, cache_k, cache_v, slot, page_table):
    bd, nh, _ = q.shape
    n_layers, n_phys, page = cache_k.shape[:3]
    n_pages = page_table.shape[1]
    assert 2 * page == MB_BLOCK and n_pages % 2 == 0
    nblk = n_pages // 2
    nsel = min(MB_TOPK, nblk)
    page_rows = page * nh
    ck = cache_k.reshape(n_layers, n_phys, page_rows, HEAD_DIM)

    def kmap(e):
        return lambda t, pt: (slot, pt[jnp.minimum(t, bd - 1), e], 0, 0)

    page_block = (None, None, page_rows, HEAD_DIM)
    vec_k = pl.BlockSpec((None, nh, HEAD_DIM), lambda t, pt: (jnp.minimum(t, bd - 1), 0, 0))
    grid_spec = pltpu.PrefetchScalarGridSpec(
        num_scalar_prefetch=1,
        grid=(bd + 1,),
        in_specs=[vec_k, vec_k, vec_k] + [pl.BlockSpec(page_block, kmap(e)) for e in range(n_pages)]
        + [pl.BlockSpec(memory_space=pl.ANY)],
        out_specs=pl.BlockSpec((None, nh, HEAD_DIM), lambda t, pt: (jnp.maximum(t - 1, 0), 0, 0)),
        scratch_shapes=[pltpu.VMEM((2, nblk, nh, MB_BLOCK, HEAD_DIM), F32),
                        pltpu.VMEM((2, nh, nsel, MB_BLOCK, HEAD_DIM), F32),
                        pltpu.VMEM((2, 3, nh, HEAD_DIM), F32),
                        pltpu.SMEM((2, nh * nsel), jnp.int32),
                        pltpu.SemaphoreType.DMA((2,))],
    )
    return pl.pallas_call(
        functools.partial(_moba_decode_kernel, scale=HEAD_DIM ** -0.5, n_pages=n_pages, slot=slot, n_samples=bd),
        grid_spec=grid_spec,
        out_shape=jax.ShapeDtypeStruct((bd, nh, HEAD_DIM), F32),
        compiler_params=_params("arbitrary"),
    )(page_table, q, k_new, v_new, *([ck] * n_pages), cache_v)


def kernel(x_prompt, x_sample, state_shortconv, state_hgrn, cache_k, cache_v, page_table, state_ffn_conv, norm_mix, norm_ffn, w_in_a, w_conv_a, w_out_a, w_in_b, lb_raw, g_norm_b, w_out_b, w_qkv_c, q_norm_c, k_norm_c, w_out_c, w_up, w_ffn_conv, b_ffn_conv, w_down):
    bp, seq, d = x_prompt.shape
    bd = x_sample.shape[0]
    assert x_sample.shape[1] == 1
    depth = norm_mix.shape[0]
    nh = d // HEAD_DIM
    past_len = page_table.shape[1] * cache_k.shape[2]
    assert past_len % MB_BLOCK == 0 and seq % MB_BLOCK == 0

    xp = x_prompt.reshape(bp * seq, d)
    xs = x_sample.reshape(bd, d)
    lb_raw3 = lb_raw.reshape(depth, nh, HEAD_DIM)
    w_in_a, w_out_a, w_in_b, w_out_b, w_qkv_c, w_out_c, w_up, w_down = (
        w.astype(BF16) for w in (w_in_a, w_out_a, w_in_b, w_out_b, w_qkv_c, w_out_c, w_up, w_down))
    w_in_b5 = w_in_b.reshape(-1, d, 4, nh, HEAD_DIM).transpose(0, 2, 3, 1, 4)
    w_out_b4 = w_out_b.reshape(-1, nh, HEAD_DIM, d)
    outs = {name: [] for name in ("sc_p", "sc_s", "hg_p", "hg_s", "kp", "vp", "ks", "vs", "fc_p", "fc_s")}
    n_conv = n_hgrn = n_moba = 0
    for i in range(depth):
        mixer = i % 3
        pre_p = pre_s = None
        if mixer == 0:
            j, n_conv = n_conv, n_conv + 1
            w1, w2 = (w_in_a, j), (w_out_a, j)
            xp, st_p = gated_conv_block_prompt(xp, norm_mix[i], w1, w_conv_a[j], None, w2, seq=seq)
            xs, st_s = gated_conv_block_decode(xs, norm_mix[i], w1, w_conv_a[j], None, w2, state_shortconv[j])
            outs["sc_p"].append(st_p)
            outs["sc_s"].append(st_s)
        elif mixer == 1:
            j, n_hgrn = n_hgrn, n_hgrn + 1
            xp, st_p = hgrn_prompt(xp, norm_mix[i], (w_in_b, j), lb_raw, g_norm_b[j], (w_out_b, j), layer=i, seq=seq)
            proj = head_proj(xs, norm_mix[i], w_in_b5, j)
            z, st_s = hgrn_decode(proj, state_hgrn, j, lb_raw3, g_norm_b[j], layer=i)
            xs = proj_residual_heads(z, (w_out_b4, j), xs)
            outs["hg_p"].append(st_p)
            outs["hg_s"].append(st_s)
        else:
            j, n_moba = n_moba, n_moba + 1
            wq, wo = (w_qkv_c, j), (w_out_c, j)
            cos_p, sin_p = _rope_tables(jnp.arange(seq, dtype=jnp.int32))
            q, k, v, km = moba_qkv(xp, norm_mix[i], wq, q_norm_c[j], k_norm_c[j], cos_p, sin_p, tm=MB_BLOCK)
            o = moba_prompt_attention(q, k, v, km.reshape(bp, seq // MB_BLOCK, d), batch=bp, seq=seq)
            cos_s, sin_s = _rope_tables(jnp.full((bd,), past_len, jnp.int32))
            qs, ks, vs, _ = moba_qkv(xs, norm_mix[i], wq, q_norm_c[j], k_norm_c[j], cos_s, sin_s, tm=bd)
            os_ = moba_decode_attention(qs.reshape(bd, nh, HEAD_DIM), ks.reshape(bd, nh, HEAD_DIM),
                                        vs.reshape(bd, nh, HEAD_DIM), cache_k, cache_v, j, page_table)
            pre_p, pre_s = (o, wo), (os_.reshape(bd, d), wo)
            outs["kp"].append(k.reshape(bp, seq, nh, HEAD_DIM))
            outs["vp"].append(v.reshape(bp, seq, nh, HEAD_DIM))
            outs["ks"].append(ks.reshape(bd, 1, nh, HEAD_DIM))
            outs["vs"].append(vs.reshape(bd, 1, nh, HEAD_DIM))
        w1, w2 = (w_up, i), (w_down, i)
        xp, st_p = gated_conv_block_prompt(xp, norm_ffn[i], w1, w_ffn_conv[i], b_ffn_conv[i], w2, seq=seq, pre=pre_p)
        xs, st_s = gated_conv_block_decode(xs, norm_ffn[i], w1, w_ffn_conv[i], b_ffn_conv[i], w2, state_ffn_conv[i],
                                           pre=pre_s)
        outs["fc_p"].append(st_p)
        outs["fc_s"].append(st_s)
    return (xp.reshape(bp, seq, d), xs.reshape(bd, 1, d), jnp.stack(outs["sc_p"]), jnp.stack(outs["sc_s"]),
            jnp.stack(outs["hg_p"]), jnp.stack(outs["hg_s"]), jnp.stack(outs["kp"]), jnp.stack(outs["vp"]),
            jnp.stack(outs["ks"]), jnp.stack(outs["vs"]), jnp.stack(outs["fc_p"]), jnp.stack(outs["fc_s"]))
```

```python
import functools
import math

import jax
import jax.numpy as jnp
from jax import lax
from jax.experimental import pallas as pl
from jax.experimental.pallas import tpu as pltpu

F32 = jnp.float32
BF16 = jnp.bfloat16

EPS = 1e-6
HEAD_DIM = 128
CONV_TAPS = 3
HG_CHUNK = 64
HG_SUB = 16
MB_BLOCK = 256
MB_TOPK = 3
ROPE_THETA = 10000.0
LOG2_E = math.log2(math.e)
V7X_VMEM_BYTES = 64 * 1024 * 1024
VMEM_LIMIT = V7X_VMEM_BYTES - 8 * 1024 * 1024

NT_DIMS = (((1,), (1,)), ((), ()))
TN_DIMS = (((0,), (0,)), ((), ()))


def _params(*sem):
    return pltpu.CompilerParams(dimension_semantics=sem, vmem_limit_bytes=VMEM_LIMIT)


def _const_spec(shape):
    n = len(shape)
    return pl.BlockSpec(shape, lambda *_: (0,) * n, pipeline_mode=pl.Buffered(1))


def _layer_spec(stack, layer):
    shape = stack.shape[1:]
    return pl.BlockSpec((None,) + shape, lambda *_: (layer,) + (0,) * len(shape), pipeline_mode=pl.Buffered(1))


def _rms(x, g):
    return x * lax.rsqrt(jnp.mean(x * x, axis=-1, keepdims=True) + EPS) * g


def _sigmoid(x):
    return 1.0 / (1.0 + jnp.exp(-x))


def _silu(x):
    return x * _sigmoid(x)


def _bdot(a, b):
    return jnp.dot(a.astype(BF16), b.astype(BF16), preferred_element_type=F32)


def _bdot_nt(a, b):
    return lax.dot_general(a.astype(BF16), b.astype(BF16), NT_DIMS, preferred_element_type=F32)


def _gcb_kernel(*refs, ffn, decode, pre, tiles_per_seq, width, cw):
    refs = list(refs)
    o_ref, wo_ref = (refs.pop(0), refs.pop(0)) if pre else (None, None)
    x_ref, g_ref, w1_ref, wc_ref = refs[:4]
    refs = refs[4:]
    bc_ref = refs.pop(0) if ffn else None
    w2_ref = refs.pop(0)
    if decode:
        sm2_ref, sm1_ref, out_ref, unew_ref, z_ref = refs
    else:
        out_ref, st_ref, z_ref, carry_ref = refs
    x = x_ref[...]
    if pre:
        x = x + _bdot(o_ref[...], wo_ref[...])
    tm = x.shape[0]
    h = _rms(x, g_ref[...]).astype(BF16)
    if not decode:
        @pl.when(pl.program_id(0) % tiles_per_seq == 0)
        def _():
            carry_ref[...] = jnp.zeros_like(carry_ref)
        row = lax.broadcasted_iota(jnp.int32, (tm, cw), 0)
    for c in range(width // cw):
        sl = pl.ds(c * cw, cw)

        def proj(j):
            return jnp.dot(h, w1_ref[:, pl.ds(j * width + c * cw, cw)], preferred_element_type=F32)

        if ffn:
            u, gate = proj(0), proj(1)
        else:
            bg = proj(0)
            u = proj(1) * proj(2)
        if decode:
            um2, um1 = sm2_ref[:, sl], sm1_ref[:, sl]
            unew_ref[:, sl] = u
        else:
            c0, c1 = carry_ref[0:1, sl], carry_ref[1:2, sl]
            um1 = jnp.where(row == 0, c1, pltpu.roll(u, 1, axis=0))
            um2 = jnp.where(row == 0, c0, jnp.where(row == 1, c1, pltpu.roll(u, 2, axis=0)))
            carry_ref[0:2, sl] = u[tm - 2:tm, :]
        wc = wc_ref[:, sl]
        conv = wc[0:1] * um2 + wc[1:2] * um1 + wc[2:3] * u
        z = _silu(conv + bc_ref[:, sl]) * gate if ffn else bg * conv
        z_ref[:, sl] = z.astype(BF16)
    out_ref[...] = x + jnp.dot(z_ref[...], w2_ref[...], preferred_element_type=F32)
    if not decode:
        st_ref[...] = carry_ref[0:2, :]


def _gcb_tile(rows):
    for tm in (512, 256, 128, 64, 32, 16, 8):
        if rows % tm == 0:
            return tm
    raise ValueError(f"row count {rows} must be a multiple of 8")


def gated_conv_block_prompt(x, g, w1, wc, bc, w2, *, seq, pre=None):
    m, d = x.shape
    width = w2[0].shape[1]
    ffn = bc is not None
    tm = _gcb_tile(seq)
    tps = seq // tm
    cw = 256
    args = [x, g.reshape(1, d), w1[0], wc] + ([bc.reshape(1, width)] if ffn else []) + [w2[0]]
    in_specs = [pl.BlockSpec((tm, d), lambda i: (i, 0)), _const_spec((1, d)), _layer_spec(*w1),
                _const_spec(wc.shape)] + ([_const_spec((1, width))] if ffn else []) + [_layer_spec(*w2)]
    if pre:
        args = [pre[0], pre[1][0]] + args
        in_specs = [pl.BlockSpec((tm, pre[0].shape[1]), lambda i: (i, 0)), _layer_spec(*pre[1])] + in_specs
    return pl.pallas_call(
        functools.partial(_gcb_kernel, ffn=ffn, decode=False, pre=bool(pre), tiles_per_seq=tps, width=width, cw=cw),
        grid=(m // tm,),
        in_specs=in_specs,
        out_specs=[pl.BlockSpec((tm, d), lambda i: (i, 0)),
                   pl.BlockSpec((None, CONV_TAPS - 1, width), lambda i: (i // tps, 0, 0))],
        out_shape=[jax.ShapeDtypeStruct((m, d), F32),
                   jax.ShapeDtypeStruct((m // seq, CONV_TAPS - 1, width), F32)],
        scratch_shapes=[pltpu.VMEM((tm, width), BF16), pltpu.VMEM((8, width), F32)],
        compiler_params=_params("arbitrary"),
    )(*args)


def gated_conv_block_decode(x, g, w1, wc, bc, w2, state, pre=None):
    m, d = x.shape
    width = w2[0].shape[1]
    ffn = bc is not None
    cw = 256
    args = [x, g.reshape(1, d), w1[0], wc] + ([bc.reshape(1, width)] if ffn else []) + [
        w2[0], state[:, 0], state[:, 1]]
    in_specs = [_const_spec(a.shape) for a in args]
    in_specs[2] = _layer_spec(*w1)
    in_specs[-3] = _layer_spec(*w2)
    if pre:
        args = [pre[0], pre[1][0]] + args
        in_specs = [_const_spec(pre[0].shape), _layer_spec(*pre[1])] + in_specs
    out, unew = pl.pallas_call(
        functools.partial(_gcb_kernel, ffn=ffn, decode=True, pre=bool(pre), tiles_per_seq=1, width=width, cw=cw),
        grid=(1,),
        in_specs=in_specs,
        out_specs=[pl.BlockSpec((m, d), lambda i: (0, 0)), pl.BlockSpec((m, width), lambda i: (0, 0))],
        out_shape=[jax.ShapeDtypeStruct((m, d), F32), jax.ShapeDtypeStruct((m, width), F32)],
        scratch_shapes=[pltpu.VMEM((m, width), BF16)],
        compiler_params=_params("arbitrary"),
    )(*args)
    return out, jnp.stack([state[:, 1], unew], axis=1)


def _lower_bound(rows, layer):
    mx = functools.reduce(jnp.maximum, rows)
    e = [jnp.exp(r - mx) for r in rows]
    tot = functools.reduce(jnp.add, e)
    w = [a / tot for a in e]
    cum = w[0]
    for j in range(1, layer + 1):
        cum = cum + w[j]
    return cum - w[0]


def _split3(x):
    hi = x.astype(BF16)
    r = x - hi.astype(F32)
    mid = r.astype(BF16)
    lo = (r - mid.astype(F32)).astype(BF16)
    return hi, mid, lo


def _level_decay(cum, f, b):
    t = cum.shape[0]
    if b >= 4:
        c3 = cum.reshape(t // (2 * b), 2 * b, HEAD_DIM)
        d = cum - jnp.broadcast_to(c3[:, b - 1:b, :], c3.shape).reshape(t, HEAD_DIM)
        return jnp.exp2(-jnp.abs(d))
    r = lax.broadcasted_iota(jnp.int32, cum.shape, 0) & (2 * b - 1)
    if b == 1:
        return jnp.where(r == 1, f, 1.0)
    return jnp.where(r == 0, pltpu.roll(f, t - 1, axis=0),
                     jnp.where(r == 1, 1.0, jnp.where(r == 2, f, pltpu.roll(f, 1, axis=0) * f)))


def _hgrn_masks(t):
    if t == 2 * HEAD_DIM:
        row = lax.broadcasted_iota(jnp.int32, (t // 2, t), 0)
        col = lax.broadcasted_iota(jnp.int32, (t // 2, t), 1) & (t // 2 - 1)
    else:
        row = lax.broadcasted_iota(jnp.int32, (t, t), 0)
        col = lax.broadcasted_iota(jnp.int32, (t, t), 1)
    return row ^ col, row > col


def _pack_halves(y):
    half = y.shape[0] // 2
    zeros = jnp.zeros((half, HEAD_DIM), y.dtype)
    return jnp.concatenate([jnp.concatenate([y[:half], zeros], axis=1),
                            jnp.concatenate([zeros, y[half:]], axis=1)], axis=0)


def _hgrn_tile(q, k, v, f, cum, st, xor_ts, below):
    t = q.shape[0]
    half = t // 2
    packed = t == 2 * HEAD_DIM
    o = _bdot_nt(q * jnp.exp2(cum), st) + jnp.sum(q * k, axis=-1, keepdims=True) * v
    vb = v.astype(BF16)

    def operands(b):
        x = _level_decay(cum, f, b)
        return (q * x).astype(BF16), (k * x).astype(BF16)

    a = None
    b = half // 2 if packed else half
    while b >= 1:
        yq, yk = operands(b)
        if packed:
            al = lax.dot_general(jnp.concatenate([yq[:half], yq[half:]], axis=1), _pack_halves(yk), NT_DIMS,
                                 preferred_element_type=F32)
        else:
            al = lax.dot_general(yq, yk, NT_DIMS, preferred_element_type=F32)
        a = al if a is None else jnp.where(xor_ts < 2 * b, al, a)
        b //= 2
    a = jnp.where(below, a, 0.0).astype(BF16)
    if packed:
        yq, yk = operands(half)
        cross = lax.dot_general(yq[half:], yk[:half], NT_DIMS, preferred_element_type=F32)
        o_p = jnp.dot(a, _pack_halves(vb), preferred_element_type=F32)
        o = o + jnp.concatenate([o_p[:, :HEAD_DIM], o_p[:, HEAD_DIM:] + _bdot(cross, vb[:half])], axis=0)
    else:
        o = o + jnp.dot(a, vb, preferred_element_type=F32)
    last = cum[t - 1:t]
    upd = lax.dot_general(v.astype(BF16), (k * jnp.exp2(last - cum)).astype(BF16), TN_DIMS,
                          preferred_element_type=F32)
    return o, st * jnp.exp2(last) + upd


def _hgrn_gates(fz, lower):
    f = lower + (1.0 - lower) * _sigmoid(fz)
    k = (1.0 - lower) * _sigmoid(-fz)
    return f, k


def _hgrn_prompt_kernel(x_ref, g_ref, w_ref, lb_ref, gn_ref, wo_ref, out_ref, st_ref,
                        s_ref, q_s, k_s, v_s, gg_s, f_s, cum_s, z_s, *, layer, tiles_per_seq):
    tm, d = x_ref.shape
    nh = d // HEAD_DIM
    t = pl.program_id(0)

    @pl.when(t % tiles_per_seq == 0)
    def _():
        s_ref[...] = jnp.zeros_like(s_ref)

    def proj(j):
        return jnp.dot(hb, w_ref[:, pl.ds(j * d, d)], preferred_element_type=F32)

    x = x_ref[...]
    hb = _rms(x, g_ref[...]).astype(BF16)
    lower = _lower_bound([lb_ref[j:j + 1, :] for j in range(lb_ref.shape[0])], layer)
    f, kk = _hgrn_gates(proj(1), lower)
    k_s[...] = kk
    f_s[...] = f
    row = lax.broadcasted_iota(jnp.int32, (tm, tm), 0)
    col = lax.broadcasted_iota(jnp.int32, (tm, tm), 1)
    tri = jnp.where(row >= col, 1.0, 0.0).astype(BF16)
    cum_s[...] = functools.reduce(jnp.add, [jnp.dot(tri, part, preferred_element_type=F32)
                                            for part in _split3(jnp.log2(f))])
    q_s[...] = _silu(proj(0))
    v_s[...] = proj(2)
    gg_s[...] = proj(3)
    xor_ts, below = _hgrn_masks(tm)
    gn = gn_ref[...]
    for hd in range(nh):
        sl = pl.ds(hd * HEAD_DIM, HEAD_DIM)
        o, st_new = _hgrn_tile(q_s[:, sl], k_s[:, sl], v_s[:, sl], f_s[:, sl], cum_s[:, sl], s_ref[hd],
                               xor_ts, below)
        s_ref[hd] = st_new
        z_s[:, sl] = (_rms(o, gn) * _silu(gg_s[:, sl])).astype(BF16)
    out_ref[...] = x + jnp.dot(z_s[...], wo_ref[...], preferred_element_type=F32)

    @pl.when(t % tiles_per_seq == tiles_per_seq - 1)
    def _():
        for hd in range(nh):
            st_ref[hd] = s_ref[hd].T


def hgrn_prompt(x, g, w_in, lb_raw, g_norm, w_out, *, layer, seq):
    m, d = x.shape
    nh = d // HEAD_DIM
    tm = min(_gcb_tile(seq), 256)
    tps = seq // tm
    return pl.pallas_call(
        functools.partial(_hgrn_prompt_kernel, layer=layer, tiles_per_seq=tps),
        grid=(m // tm,),
        in_specs=[pl.BlockSpec((tm, d), lambda i: (i, 0)), _const_spec((1, d)), _layer_spec(*w_in),
                  _const_spec(lb_raw.shape), _const_spec((1, HEAD_DIM)), _layer_spec(*w_out)],
        out_specs=[pl.BlockSpec((tm, d), lambda i: (i, 0)),
                   pl.BlockSpec((None, nh, HEAD_DIM, HEAD_DIM), lambda i: (i // tps, 0, 0, 0))],
        out_shape=[jax.ShapeDtypeStruct((m, d), F32),
                   jax.ShapeDtypeStruct((m // seq, nh, HEAD_DIM, HEAD_DIM), F32)],
        scratch_shapes=[pltpu.VMEM((nh, HEAD_DIM, HEAD_DIM), F32)] + [pltpu.VMEM((tm, d), F32)] * 6
        + [pltpu.VMEM((tm, d), BF16)],
        compiler_params=_params("arbitrary"),
    )(x, g.reshape(1, d), w_in[0], lb_raw, g_norm.reshape(1, HEAD_DIM), w_out[0])


def _column_replicate(row, n_cols):
    eye = lax.broadcasted_iota(jnp.int32, (HEAD_DIM, HEAD_DIM), 0) == lax.broadcasted_iota(
        jnp.int32, (HEAD_DIM, HEAD_DIM), 1)
    hi = jnp.where(eye, row, 0.0).astype(BF16)
    lo = jnp.where(eye, row - row.astype(BF16).astype(F32), 0.0).astype(BF16)
    rep = jnp.dot(jnp.concatenate([hi, lo], axis=0), jnp.ones((HEAD_DIM, n_cols), BF16), preferred_element_type=F32)
    return rep[:HEAD_DIM] + rep[HEAD_DIM:]


def _hgrn_decode_layer_kernel(x_ref, g_ref, wq_ref, wf_ref, wi_ref, wg_ref, s_ref, lb_ref, gn_ref, wo_ref,
                              out_ref, sn_ref, o_s, *, layer):
    bd = x_ref.shape[0]
    hd = pl.program_id(0)
    x = x_ref[...]
    hb = _rms(x, g_ref[...]).astype(BF16)
    lower_row = _lower_bound([lb_ref[pl.ds(j, 1), :] for j in range(lb_ref.shape[0])], layer)
    lower = _column_replicate(lower_row, bd)
    f_t, k_t = _hgrn_gates(lax.dot_general(wf_ref[...], hb, NT_DIMS, preferred_element_type=F32), lower)
    q_t = _silu(lax.dot_general(wq_ref[...], hb, NT_DIMS, preferred_element_type=F32))
    v = jnp.dot(hb, wi_ref[...], preferred_element_type=F32)
    for b in range(bd):
        col = slice(b, b + 1)
        s_new = (jnp.broadcast_to(f_t[:, col], (HEAD_DIM, HEAD_DIM)) * s_ref[b]
                 + jnp.broadcast_to(k_t[:, col], (HEAD_DIM, HEAD_DIM)) * v[col, :])
        sn_ref[b] = s_new
        o_s[col, :] = jnp.sum(jnp.broadcast_to(q_t[:, col], (HEAD_DIM, HEAD_DIM)) * s_new, axis=0, keepdims=True)
    z = _rms(o_s[...], gn_ref[...]) * _silu(jnp.dot(hb, wg_ref[...], preferred_element_type=F32))

    @pl.when(hd == 0)
    def _():
        out_ref[...] = x

    out_ref[...] += _bdot(z, wo_ref[...])


def hgrn_decode_layer(x, g, w_in, w_in_t, w_out4, slot, states, lb_raw3, g_norm, *, layer):
    bd, d = x.shape
    nh = d // HEAD_DIM
    depth = lb_raw3.shape[0]

    def head_cols(group):
        return pl.BlockSpec((None, d, HEAD_DIM), lambda h: (slot, 0, group * nh + h))

    def head_rows(group):
        return pl.BlockSpec((None, HEAD_DIM, d), lambda h: (slot, group * nh + h, 0))

    state_spec = pl.BlockSpec((bd, None, HEAD_DIM, HEAD_DIM), lambda h: (0, h, 0, 0))
    return pl.pallas_call(
        functools.partial(_hgrn_decode_layer_kernel, layer=layer),
        grid=(nh,),
        in_specs=[_const_spec((bd, d)), _const_spec((1, d)), head_rows(0), head_rows(1), head_cols(2), head_cols(3),
                  pl.BlockSpec((None, bd, None, HEAD_DIM, HEAD_DIM), lambda h: (slot, 0, h, 0, 0)),
                  pl.BlockSpec((None, depth, HEAD_DIM), lambda h: (h, 0, 0)), _const_spec((1, HEAD_DIM)),
                  pl.BlockSpec((None, None, HEAD_DIM, d), lambda h: (slot, h, 0, 0))],
        out_specs=[pl.BlockSpec((bd, d), lambda h: (0, 0)), state_spec],
        out_shape=[jax.ShapeDtypeStruct((bd, d), F32), jax.ShapeDtypeStruct(states.shape[1:], F32)],
        scratch_shapes=[pltpu.VMEM((bd, HEAD_DIM), F32)],
        compiler_params=_params("arbitrary"),
    )(x, g.reshape(1, d), w_in_t, w_in_t, w_in, w_in, states, lb_raw3.transpose(1, 0, 2),
      g_norm.reshape(1, HEAD_DIM), w_out4)


def _rope_tables(pos):
    half = HEAD_DIM // 2
    inv = jnp.exp(-math.log(ROPE_THETA) * jnp.arange(half, dtype=F32) / half)
    ang = pos.astype(F32)[:, None] * inv[None, :]
    cos, sin = jnp.cos(ang), jnp.sin(ang)
    return jnp.concatenate([cos, cos], axis=-1), jnp.concatenate([-sin, sin], axis=-1)


def _qkv_kernel(x_ref, g_ref, w_ref, qn_ref, kn_ref, cos_ref, sin_ref, q_ref, k_ref, v_ref, km_ref):
    d = x_ref.shape[1]
    hb = _rms(x_ref[...], g_ref[...]).astype(BF16)
    cos, sin = cos_ref[...], sin_ref[...]

    def norm_rope(y, gain):
        y = _rms(y, gain)
        return y * cos + pltpu.roll(y, HEAD_DIM // 2, axis=1) * sin

    v_ref[...] = jnp.dot(hb, w_ref[:, pl.ds(2 * d, d)], preferred_element_type=F32)
    for j, (out_ref, gain_ref) in enumerate(((q_ref, qn_ref), (k_ref, kn_ref))):
        y = jnp.dot(hb, w_ref[:, pl.ds(j * d, d)], preferred_element_type=F32)
        for hd in range(d // HEAD_DIM):
            sl = pl.ds(hd * HEAD_DIM, HEAD_DIM)
            yh = norm_rope(y[:, hd * HEAD_DIM:(hd + 1) * HEAD_DIM], gain_ref[...])
            out_ref[:, sl] = yh
            if j == 1:
                km_ref[:, sl] = jnp.mean(yh, axis=0, keepdims=True)


def moba_qkv(x, g, w, qn, kn, cosf, sinf, *, tm):
    m, d = x.shape
    tps = cosf.shape[0] // tm
    return pl.pallas_call(
        _qkv_kernel,
        grid=(m // tm,),
        in_specs=[pl.BlockSpec((tm, d), lambda i: (i, 0)), _const_spec((1, d)), _layer_spec(*w),
                  _const_spec((1, HEAD_DIM)), _const_spec((1, HEAD_DIM)),
                  pl.BlockSpec((tm, HEAD_DIM), lambda i: (i % tps, 0)),
                  pl.BlockSpec((tm, HEAD_DIM), lambda i: (i % tps, 0))],
        out_specs=[pl.BlockSpec((tm, d), lambda i: (i, 0))] * 3 + [pl.BlockSpec((None, 1, d), lambda i: (i, 0, 0))],
        out_shape=[jax.ShapeDtypeStruct((m, d), F32)] * 3 + [jax.ShapeDtypeStruct((m // tm, 1, d), F32)],
        compiler_params=_params("arbitrary"),
    )(x, g.reshape(1, d), w[0], qn.reshape(1, HEAD_DIM), kn.reshape(1, HEAD_DIM), cosf, sinf)


def _top_blocks(gate, n_idx, n_valid, axis):
    neg = -jnp.inf
    g = jnp.where(n_idx < n_valid, gate, neg)
    sel = jnp.zeros(gate.shape, F32)
    big = jnp.int32(2 ** 30)
    for _ in range(MB_TOPK):
        m = jnp.max(g, axis=axis, keepdims=True)
        idx = jnp.min(jnp.where(g == m, n_idx, big), axis=axis, keepdims=True)
        hit = n_idx == idx
        sel = jnp.where(hit, jnp.where(m > neg, 1.0, sel), sel)
        g = jnp.where(hit, neg, g)
    return sel


def _moba_prompt_kernel(q_ref, k_ref, v_ref, km_ref, o_ref, kb_ref, vt_ref, *, scale):
    nblk = km_ref.shape[0]
    blk = q_ref.shape[0] // nblk
    for n in range(nblk):
        rows = pl.ds(n * blk, blk)
        kb_ref[n] = k_ref[rows, :].astype(BF16)
        vt_ref[:, rows] = v_ref[rows, :].T.astype(BF16)
    km = km_ref[...]
    n_idx = lax.broadcasted_iota(jnp.int32, (nblk, blk), 0)
    causal = lax.broadcasted_iota(jnp.int32, (blk, blk), 0) <= lax.broadcasted_iota(jnp.int32, (blk, blk), 1)
    for i in range(nblk):
        q = q_ref[pl.ds(i * blk, blk), :]
        qb = (q * scale).astype(BF16)

        def scores(n):
            return lax.dot_general(kb_ref[n], qb, NT_DIMS, preferred_element_type=F32)

        s = [None] * i + [jnp.where(causal, scores(i), -jnp.inf)]
        if i > 0:
            gate = lax.dot_general(km, q, NT_DIMS, precision=lax.Precision.HIGHEST, preferred_element_type=F32)
            sel = _top_blocks(gate, n_idx, i, 0)
            for n in range(i):
                s[n] = jnp.where(sel[n:n + 1, :] > 0.5, scores(n), -jnp.inf)
        m = functools.reduce(jnp.maximum, [jnp.max(sn, axis=0, keepdims=True) for sn in s])
        p = [jnp.exp2(sn - m) for sn in s]
        l = functools.reduce(jnp.add, [jnp.sum(pn, axis=0, keepdims=True) for pn in p])
        pcat = jnp.concatenate([pn.astype(BF16) for pn in p], axis=0)
        acc = jnp.dot(vt_ref[:, pl.ds(0, (i + 1) * blk)], pcat, preferred_element_type=F32)
        o_ref[pl.ds(i * blk, blk), :] = (acc / l).T


def moba_prompt_attention(q, k, v, kmeans, *, batch, seq):
    m, d = q.shape
    nh = d // HEAD_DIM
    nblk = seq // MB_BLOCK
    return pl.pallas_call(
        functools.partial(_moba_prompt_kernel, scale=HEAD_DIM ** -0.5 * LOG2_E),
        grid=(batch, nh),
        in_specs=[pl.BlockSpec((seq, HEAD_DIM), lambda b, h: (b, h))] * 3
        + [pl.BlockSpec((None, nblk, HEAD_DIM), lambda b, h: (b, 0, h))],
        out_specs=pl.BlockSpec((seq, HEAD_DIM), lambda b, h: (b, h)),
        out_shape=jax.ShapeDtypeStruct((m, d), F32),
        scratch_shapes=[pltpu.VMEM((nblk, MB_BLOCK, HEAD_DIM), BF16), pltpu.VMEM((HEAD_DIM, seq), BF16)],
        compiler_params=_params("arbitrary", "arbitrary"),
    )(q, k, v, kmeans)


def _top_block_ids(gate, n_idx, n_take):
    g = gate
    ids = []
    for _ in range(n_take):
        m = jnp.max(g, axis=1, keepdims=True)
        idx = jnp.min(jnp.where(g == m, n_idx, jnp.int32(2 ** 30)), axis=1, keepdims=True)
        ids.append(idx)
        g = jnp.where(n_idx == idx, -jnp.inf, g)
    return ids


def _moba_decode_kernel(pt_ref, q_ref, kn_ref, vn_ref, *rest, scale, n_pages, slot, n_samples):
    k_refs = rest[:n_pages]
    cv_hbm, o_ref, s_ref, vbuf, stat_ref, ids_ref, sems = rest[n_pages:]
    nh = q_ref.shape[0]
    nblk = n_pages // 2
    nsel = min(MB_TOPK, nblk)
    page = k_refs[0].shape[0] // nh
    t = pl.program_id(0)
    par = t % 2

    def v_copy(p, b, h, r, e):
        n = ids_ref[p, h * nsel + r]
        return pltpu.make_async_copy(cv_hbm.at[slot, pt_ref[b, 2 * n + e], :, h, :],
                                     vbuf.at[p, h, r, pl.ds(e * page, page), :], sems.at[p])

    def v_copies(p, b):
        return [v_copy(p, b, h, r, e) for h in range(nh) for r in range(nsel) for e in range(2)]

    @pl.when(t < n_samples)
    def _():
        q = q_ref[...]
        lane = lax.broadcasted_iota(jnp.int32, (nh, HEAD_DIM), 1)
        ones = jnp.ones((HEAD_DIM, HEAD_DIM), BF16)
        gates = jnp.full((nh, HEAD_DIM), -jnp.inf, F32)
        mx = []
        for n in range(nblk):
            ksum, mxn = [], []
            for h in range(nh):
                qh = q[h:h + 1, :] * scale
                ks = jnp.zeros((1, HEAD_DIM), F32)
                mh = jnp.full((1, HEAD_DIM), -jnp.inf, F32)
                for e in range(2):
                    kh = k_refs[2 * n + e][pl.ds(h, page, stride=nh), :]
                    sc = jnp.dot((kh * qh).astype(BF16), ones, preferred_element_type=F32)
                    s_ref[par, n, h, pl.ds(e * page, page), :] = sc
                    ks = ks + jnp.sum(kh, axis=0, keepdims=True)
                    mh = jnp.maximum(mh, jnp.max(sc, axis=0, keepdims=True))
                ksum.append(ks)
                mxn.append(mh)
            mx.append(jnp.concatenate(mxn, axis=0))
            mean = jnp.concatenate(ksum, axis=0) / (2 * page)
            gates = jnp.where(lane == n, jnp.sum(q * mean, axis=1, keepdims=True), gates)
        ids = _top_block_ids(gates, lane, nsel)
        s_own = jnp.sum(q * kn_ref[...], axis=1, keepdims=True) * scale
        m = jnp.broadcast_to(s_own, (nh, HEAD_DIM))
        ids_b = [jnp.broadcast_to(idx, (nh, HEAD_DIM)) for idx in ids]
        for n in range(nblk):
            mx_n = -jnp.inf
            for idx in ids_b:
                mx_n = jnp.where(idx == n, mx[n], mx_n)
            m = jnp.maximum(m, mx_n)
        p_own = jnp.exp2(s_own - m)
        stat_ref[par, 0] = m
        stat_ref[par, 1] = p_own
        stat_ref[par, 2] = p_own * vn_ref[...]
        for r in range(nsel):
            for h in range(nh):
                ids_ref[par, h * nsel + r] = ids[r][h, 0]
        for cp in v_copies(par, t):
            cp.start()

    @pl.when(t >= 1)
    def _():
        prev = 1 - par
        for cp in v_copies(prev, t - 1):
            cp.wait()
        m = stat_ref[prev, 0]
        l_rows, acc_rows = [], []
        for h in range(nh):
            l = jnp.zeros((1, HEAD_DIM), F32)
            acc = jnp.zeros((1, HEAD_DIM), F32)
            for r in range(nsel):
                p = jnp.exp2(s_ref[prev, ids_ref[prev, h * nsel + r], h] - m[h:h + 1, :])
                l = l + jnp.sum(p, axis=0, keepdims=True)
                acc = acc + jnp.sum(p * vbuf[prev, h, r], axis=0, keepdims=True)
            l_rows.append(l)
            acc_rows.append(acc)
        l = stat_ref[prev, 1] + jnp.concatenate(l_rows, axis=0)
        o_ref[...] = (stat_ref[prev, 2] + jnp.concatenate(acc_rows, axis=0)) / l


def moba_decode_attention(q, k_new, v_new, cache_k, cache_v, slot, page_table):
    bd, nh, _ = q.shape
    n_layers, n_phys, page = cache_k.shape[:3]
    n_pages = page_table.shape[1]
    assert 2 * page == MB_BLOCK and n_pages % 2 == 0
    nblk = n_pages // 2
    nsel = min(MB_TOPK, nblk)
    page_rows = page * nh
    ck = cache_k.reshape(n_layers, n_phys, page_rows, HEAD_DIM)

    def kmap(e):
        return lambda t, pt: (slot, pt[jnp.minimum(t, bd - 1), e], 0, 0)

    page_block = (None, None, page_rows, HEAD_DIM)
    vec_k = pl.BlockSpec((None, nh, HEAD_DIM), lambda t, pt: (jnp.minimum(t, bd - 1), 0, 0))
    grid_spec = pltpu.PrefetchScalarGridSpec(
        num_scalar_prefetch=1,
        grid=(bd + 1,),
        in_specs=[vec_k, vec_k, vec_k] + [pl.BlockSpec(page_block, kmap(e)) for e in range(n_pages)]
        + [pl.BlockSpec(memory_space=pl.ANY)],
        out_specs=pl.BlockSpec((None, nh, HEAD_DIM), lambda t, pt: (jnp.maximum(t - 1, 0), 0, 0)),
        scratch_shapes=[pltpu.VMEM((2, nblk, nh, MB_BLOCK, HEAD_DIM), F32),
                        pltpu.VMEM((2, nh, nsel, MB_BLOCK, HEAD_DIM), F32),
                        pltpu.VMEM((2, 3, nh, HEAD_DIM), F32),
                        pltpu.SMEM((2, nh * nsel), jnp.int32),
                        pltpu.SemaphoreType.DMA((2,))],
    )
    return pl.pallas_call(
        functools.partial(_moba_decode_kernel, scale=HEAD_DIM ** -0.5 * LOG2_E, n_pages=n_pages, slot=slot,
                          n_samples=bd),
        grid_spec=grid_spec,
        out_shape=jax.ShapeDtypeStruct((bd, nh, HEAD_DIM), F32),
        compiler_params=_params("arbitrary"),
    )(page_table, q, k_new, v_new, *([ck] * n_pages), cache_v)


def kernel(x_prompt, x_sample, state_shortconv, state_hgrn, cache_k, cache_v, page_table, state_ffn_conv, norm_mix, norm_ffn, w_in_a, w_conv_a, w_out_a, w_in_b, lb_raw, g_norm_b, w_out_b, w_qkv_c, q_norm_c, k_norm_c, w_out_c, w_up, w_ffn_conv, b_ffn_conv, w_down):
    bp, seq, d = x_prompt.shape
    bd = x_sample.shape[0]
    assert x_sample.shape[1] == 1
    depth = norm_mix.shape[0]
    nh = d // HEAD_DIM
    past_len = page_table.shape[1] * cache_k.shape[2]
    assert past_len % MB_BLOCK == 0 and seq % MB_BLOCK == 0

    xp = x_prompt.reshape(bp * seq, d)
    xs = x_sample.reshape(bd, d)
    lb_raw3 = lb_raw.reshape(depth, nh, HEAD_DIM)
    w_in_a, w_out_a, w_in_b, w_out_b, w_qkv_c, w_out_c, w_up, w_down = (
        w.astype(BF16) for w in (w_in_a, w_out_a, w_in_b, w_out_b, w_qkv_c, w_out_c, w_up, w_down))
    w_in_b_t = w_in_b[:, :, :2 * d].transpose(0, 2, 1)
    w_out_b4 = w_out_b.reshape(-1, nh, HEAD_DIM, d)
    outs = {name: [] for name in ("sc_p", "sc_s", "hg_p", "hg_s", "kp", "vp", "ks", "vs", "fc_p", "fc_s")}
    n_conv = n_hgrn = n_moba = 0
    for i in range(depth):
        mixer = i % 3
        pre_p = pre_s = None
        if mixer == 0:
            j, n_conv = n_conv, n_conv + 1
            w1, w2 = (w_in_a, j), (w_out_a, j)
            xp, st_p = gated_conv_block_prompt(xp, norm_mix[i], w1, w_conv_a[j], None, w2, seq=seq)
            xs, st_s = gated_conv_block_decode(xs, norm_mix[i], w1, w_conv_a[j], None, w2, state_shortconv[j])
            outs["sc_p"].append(st_p)
            outs["sc_s"].append(st_s)
        elif mixer == 1:
            j, n_hgrn = n_hgrn, n_hgrn + 1
            xp, st_p = hgrn_prompt(xp, norm_mix[i], (w_in_b, j), lb_raw, g_norm_b[j], (w_out_b, j), layer=i, seq=seq)
            xs, st_s = hgrn_decode_layer(xs, norm_mix[i], w_in_b, w_in_b_t, w_out_b4, j, state_hgrn, lb_raw3,
                                         g_norm_b[j], layer=i)
            outs["hg_p"].append(st_p)
            outs["hg_s"].append(st_s)
        else:
            j, n_moba = n_moba, n_moba + 1
            wq, wo = (w_qkv_c, j), (w_out_c, j)
            cos_p, sin_p = _rope_tables(jnp.arange(seq, dtype=jnp.int32))
            q, k, v, km = moba_qkv(xp, norm_mix[i], wq, q_norm_c[j], k_norm_c[j], cos_p, sin_p, tm=MB_BLOCK)
            o = moba_prompt_attention(q, k, v, km.reshape(bp, seq // MB_BLOCK, d), batch=bp, seq=seq)
            cos_s, sin_s = _rope_tables(jnp.full((bd,), past_len, jnp.int32))
            qs, ks, vs, _ = moba_qkv(xs, norm_mix[i], wq, q_norm_c[j], k_norm_c[j], cos_s, sin_s, tm=bd)
            os_ = moba_decode_attention(qs.reshape(bd, nh, HEAD_DIM), ks.reshape(bd, nh, HEAD_DIM),
                                        vs.reshape(bd, nh, HEAD_DIM), cache_k, cache_v, j, page_table)
            pre_p, pre_s = (o, wo), (os_.reshape(bd, d), wo)
            outs["kp"].append(k.reshape(bp, seq, nh, HEAD_DIM))
            outs["vp"].append(v.reshape(bp, seq, nh, HEAD_DIM))
            outs["ks"].append(ks.reshape(bd, 1, nh, HEAD_DIM))
            outs["vs"].append(vs.reshape(bd, 1, nh, HEAD_DIM))
        w1, w2 = (w_up, i), (w_down, i)
        xp, st_p = gated_conv_block_prompt(xp, norm_ffn[i], w1, w_ffn_conv[i], b_ffn_conv[i], w2, seq=seq, pre=pre_p)
        xs, st_s = gated_conv_block_decode(xs, norm_ffn[i], w1, w_ffn_conv[i], b_ffn_conv[i], w2, state_ffn_conv[i],
                                           pre=pre_s)
        outs["fc_p"].append(st_p)
        outs["fc_s"].append(st_s)
    return (xp.reshape(bp, seq, d), xs.reshape(bd, 1, d), jnp.stack(outs["sc_p"]), jnp.stack(outs["sc_s"]),
            jnp.stack(outs["hg_p"]), jnp.stack(outs["hg_s"]), jnp.stack(outs["kp"]), jnp.stack(outs["vp"]),
            jnp.stack(outs["ks"]), jnp.stack(outs["vs"]), jnp.stack(outs["fc_p"]), jnp.stack(outs["fc_s"]))
```

```python
import functools
import math

import jax
import jax.numpy as jnp
from jax import lax
from jax.experimental import pallas as pl
from jax.experimental.pallas import tpu as pltpu

F32 = jnp.float32
BF16 = jnp.bfloat16

EPS = 1e-6
HEAD_DIM = 128
CONV_TAPS = 3
HG_CHUNK = 64
HG_SUB = 16
MB_BLOCK = 256
MB_TOPK = 3
ROPE_THETA = 10000.0
LOG2_E = math.log2(math.e)
V7X_VMEM_BYTES = 64 * 1024 * 1024
VMEM_LIMIT = V7X_VMEM_BYTES - 8 * 1024 * 1024

NT_DIMS = (((1,), (1,)), ((), ()))
TN_DIMS = (((0,), (0,)), ((), ()))


def _params(*sem):
    return pltpu.CompilerParams(dimension_semantics=sem, vmem_limit_bytes=VMEM_LIMIT)


def _const_spec(shape):
    n = len(shape)
    return pl.BlockSpec(shape, lambda *_: (0,) * n, pipeline_mode=pl.Buffered(1))


def _layer_spec(stack, layer):
    shape = stack.shape[1:]
    return pl.BlockSpec((None,) + shape, lambda *_: (layer,) + (0,) * len(shape), pipeline_mode=pl.Buffered(1))


def _rms(x, g):
    return x * lax.rsqrt(jnp.mean(x * x, axis=-1, keepdims=True) + EPS) * g


def _sigmoid(x):
    return 1.0 / (1.0 + jnp.exp(-x))


def _silu(x):
    return x * _sigmoid(x)


def _bdot(a, b):
    return jnp.dot(a.astype(BF16), b.astype(BF16), preferred_element_type=F32)


def _bdot_nt(a, b):
    return lax.dot_general(a.astype(BF16), b.astype(BF16), NT_DIMS, preferred_element_type=F32)


def _gcb_rows(x, taps, g_ref, w1_ref, wc_ref, bc_ref, w2_ref, z_ref, *, width, cw):
    ffn = bc_ref is not None
    h = _rms(x, g_ref[...]).astype(BF16)
    for c in range(width // cw):
        sl = pl.ds(c * cw, cw)

        def proj(j):
            return jnp.dot(h, w1_ref[:, pl.ds(j * width + c * cw, cw)], preferred_element_type=F32)

        if ffn:
            u, gate = proj(0), proj(1)
        else:
            bg = proj(0)
            u = proj(1) * proj(2)
        um2, um1 = taps(sl, u)
        wc = wc_ref[:, sl]
        conv = wc[0:1] * um2 + wc[1:2] * um1 + wc[2:3] * u
        z = _silu(conv + bc_ref[:, sl]) * gate if ffn else bg * conv
        z_ref[:, sl] = z.astype(BF16)
    return x + jnp.dot(z_ref[...], w2_ref[...], preferred_element_type=F32)


def _gcb_kernel(*refs, ffn, pre, n_tiles, tiles_per_seq, width, cw):
    refs = list(refs)
    o_ref, os_ref, wo_ref = (refs.pop(0), refs.pop(0), refs.pop(0)) if pre else (None, None, None)
    x_ref, xs_ref, sm2_ref, sm1_ref, g_ref, w1_ref, wc_ref = refs[:7]
    refs = refs[7:]
    bc_ref = refs.pop(0) if ffn else None
    w2_ref, out_ref, st_ref, outs_ref, unew_ref, z_ref, zs_ref, carry_ref = refs
    step = pl.program_id(0)
    weights = (g_ref, w1_ref, wc_ref, bc_ref, w2_ref)

    def with_pre(x, o):
        return x + _bdot(o[...], wo_ref[...]) if pre else x

    @pl.when(step < n_tiles)
    def _():
        tm = x_ref.shape[0]

        @pl.when(step % tiles_per_seq == 0)
        def _():
            carry_ref[...] = jnp.zeros_like(carry_ref)

        row = lax.broadcasted_iota(jnp.int32, (tm, cw), 0)

        def taps(sl, u):
            c0, c1 = carry_ref[0:1, sl], carry_ref[1:2, sl]
            um1 = jnp.where(row == 0, c1, pltpu.roll(u, 1, axis=0))
            um2 = jnp.where(row == 0, c0, jnp.where(row == 1, c1, pltpu.roll(u, 2, axis=0)))
            carry_ref[0:2, sl] = u[tm - 2:tm, :]
            return um2, um1

        out_ref[...] = _gcb_rows(with_pre(x_ref[...], o_ref), taps, *weights, z_ref, width=width, cw=cw)
        st_ref[...] = carry_ref[0:2, :]

    @pl.when(step == n_tiles)
    def _():
        def taps(sl, u):
            unew_ref[:, sl] = u
            return sm2_ref[:, sl], sm1_ref[:, sl]

        outs_ref[...] = _gcb_rows(with_pre(xs_ref[...], os_ref), taps, *weights, zs_ref, width=width, cw=cw)


def _gcb_tile(rows):
    for tm in (512, 256, 128, 64, 32, 16, 8):
        if rows % tm == 0:
            return tm
    raise ValueError(f"row count {rows} must be a multiple of 8")


def gated_conv_block(x, xs, state, g, w1, wc, bc, w2, *, seq, pre=None):
    m, d = x.shape
    bd = xs.shape[0]
    width = w2[0].shape[1]
    ffn = bc is not None
    tm = _gcb_tile(seq)
    tps = seq // tm
    n_tiles = m // tm
    cw = 256

    def tile(i):
        return jnp.minimum(i, n_tiles - 1)

    args = [x, xs, state[:, 0], state[:, 1], g.reshape(1, d), w1[0], wc] + (
        [bc.reshape(1, width)] if ffn else []) + [w2[0]]
    in_specs = [pl.BlockSpec((tm, d), lambda i: (tile(i), 0)), _const_spec((bd, d)), _const_spec((bd, width)),
                _const_spec((bd, width)), _const_spec((1, d)), _layer_spec(*w1), _const_spec(wc.shape)] + (
        [_const_spec((1, width))] if ffn else []) + [_layer_spec(*w2)]
    if pre:
        o, os_, wo = pre
        args = [o, os_, wo[0]] + args
        in_specs = [pl.BlockSpec((tm, o.shape[1]), lambda i: (tile(i), 0)), _const_spec(os_.shape),
                    _layer_spec(*wo)] + in_specs
    out, st, outs, unew = pl.pallas_call(
        functools.partial(_gcb_kernel, ffn=ffn, pre=bool(pre), n_tiles=n_tiles, tiles_per_seq=tps, width=width,
                          cw=cw),
        grid=(n_tiles + 1,),
        in_specs=in_specs,
        out_specs=[pl.BlockSpec((tm, d), lambda i: (tile(i), 0)),
                   pl.BlockSpec((None, CONV_TAPS - 1, width), lambda i: (tile(i) // tps, 0, 0)),
                   pl.BlockSpec((bd, d), lambda i: (0, 0)), pl.BlockSpec((bd, width), lambda i: (0, 0))],
        out_shape=[jax.ShapeDtypeStruct((m, d), F32), jax.ShapeDtypeStruct((m // seq, CONV_TAPS - 1, width), F32),
                   jax.ShapeDtypeStruct((bd, d), F32), jax.ShapeDtypeStruct((bd, width), F32)],
        scratch_shapes=[pltpu.VMEM((tm, width), BF16), pltpu.VMEM((bd, width), BF16), pltpu.VMEM((8, width), F32)],
        compiler_params=_params("arbitrary"),
    )(*args)
    return out, st, outs, jnp.stack([state[:, 1], unew], axis=1)


def _lower_bound(rows, layer):
    mx = functools.reduce(jnp.maximum, rows)
    e = [jnp.exp(r - mx) for r in rows]
    tot = functools.reduce(jnp.add, e)
    w = [a / tot for a in e]
    cum = w[0]
    for j in range(1, layer + 1):
        cum = cum + w[j]
    return cum - w[0]


def _split3(x):
    hi = x.astype(BF16)
    r = x - hi.astype(F32)
    mid = r.astype(BF16)
    lo = (r - mid.astype(F32)).astype(BF16)
    return hi, mid, lo


def _level_decay(cum, f, b):
    t = cum.shape[0]
    if b >= 4:
        c3 = cum.reshape(t // (2 * b), 2 * b, HEAD_DIM)
        d = cum - jnp.broadcast_to(c3[:, b - 1:b, :], c3.shape).reshape(t, HEAD_DIM)
        return jnp.exp2(-jnp.abs(d))
    r = lax.broadcasted_iota(jnp.int32, cum.shape, 0) & (2 * b - 1)
    if b == 1:
        return jnp.where(r == 1, f, 1.0)
    return jnp.where(r == 0, pltpu.roll(f, t - 1, axis=0),
                     jnp.where(r == 1, 1.0, jnp.where(r == 2, f, pltpu.roll(f, 1, axis=0) * f)))


def _hgrn_masks(t):
    if t == 2 * HEAD_DIM:
        row = lax.broadcasted_iota(jnp.int32, (t // 2, t), 0)
        col = lax.broadcasted_iota(jnp.int32, (t // 2, t), 1) & (t // 2 - 1)
    else:
        row = lax.broadcasted_iota(jnp.int32, (t, t), 0)
        col = lax.broadcasted_iota(jnp.int32, (t, t), 1)
    return row ^ col, row > col


def _pack_halves(y):
    half = y.shape[0] // 2
    zeros = jnp.zeros((half, HEAD_DIM), y.dtype)
    return jnp.concatenate([jnp.concatenate([y[:half], zeros], axis=1),
                            jnp.concatenate([zeros, y[half:]], axis=1)], axis=0)


def _hgrn_tile(q, k, v, f, cum, st, xor_ts, below):
    t = q.shape[0]
    half = t // 2
    packed = t == 2 * HEAD_DIM
    o = _bdot_nt(q * jnp.exp2(cum), st) + jnp.sum(q * k, axis=-1, keepdims=True) * v
    vb = v.astype(BF16)

    def operands(b):
        x = _level_decay(cum, f, b)
        return (q * x).astype(BF16), (k * x).astype(BF16)

    a = None
    b = half // 2 if packed else half
    while b >= 1:
        yq, yk = operands(b)
        if packed:
            al = lax.dot_general(jnp.concatenate([yq[:half], yq[half:]], axis=1), _pack_halves(yk), NT_DIMS,
                                 preferred_element_type=F32)
        else:
            al = lax.dot_general(yq, yk, NT_DIMS, preferred_element_type=F32)
        a = al if a is None else jnp.where(xor_ts < 2 * b, al, a)
        b //= 2
    a = jnp.where(below, a, 0.0).astype(BF16)
    if packed:
        yq, yk = operands(half)
        cross = lax.dot_general(yq[half:], yk[:half], NT_DIMS, preferred_element_type=F32)
        o_p = jnp.dot(a, _pack_halves(vb), preferred_element_type=F32)
        o = o + jnp.concatenate([o_p[:, :HEAD_DIM], o_p[:, HEAD_DIM:] + _bdot(cross, vb[:half])], axis=0)
    else:
        o = o + jnp.dot(a, vb, preferred_element_type=F32)
    last = cum[t - 1:t]
    upd = lax.dot_general(v.astype(BF16), (k * jnp.exp2(last - cum)).astype(BF16), TN_DIMS,
                          preferred_element_type=F32)
    return o, st * jnp.exp2(last) + upd


def _hgrn_gates(fz, lower):
    f = lower + (1.0 - lower) * _sigmoid(fz)
    k = (1.0 - lower) * _sigmoid(-fz)
    return f, k


def _hgrn_prompt_kernel(x_ref, g_ref, w_ref, lb_ref, gn_ref, wo_ref, out_ref, st_ref,
                        s_ref, q_s, k_s, v_s, gg_s, f_s, cum_s, z_s, *, layer, tiles_per_seq):
    tm, d = x_ref.shape
    nh = d // HEAD_DIM
    t = pl.program_id(0)

    @pl.when(t % tiles_per_seq == 0)
    def _():
        s_ref[...] = jnp.zeros_like(s_ref)

    def proj(j):
        return jnp.dot(hb, w_ref[:, pl.ds(j * d, d)], preferred_element_type=F32)

    x = x_ref[...]
    hb = _rms(x, g_ref[...]).astype(BF16)
    lower = _lower_bound([lb_ref[j:j + 1, :] for j in range(lb_ref.shape[0])], layer)
    f, kk = _hgrn_gates(proj(1), lower)
    k_s[...] = kk
    f_s[...] = f
    row = lax.broadcasted_iota(jnp.int32, (tm, tm), 0)
    col = lax.broadcasted_iota(jnp.int32, (tm, tm), 1)
    tri = jnp.where(row >= col, 1.0, 0.0).astype(BF16)
    cum_s[...] = functools.reduce(jnp.add, [jnp.dot(tri, part, preferred_element_type=F32)
                                            for part in _split3(jnp.log2(f))])
    q_s[...] = _silu(proj(0))
    v_s[...] = proj(2)
    gg_s[...] = proj(3)
    xor_ts, below = _hgrn_masks(tm)
    gn = gn_ref[...]
    for hd in range(nh):
        sl = pl.ds(hd * HEAD_DIM, HEAD_DIM)
        o, st_new = _hgrn_tile(q_s[:, sl], k_s[:, sl], v_s[:, sl], f_s[:, sl], cum_s[:, sl], s_ref[hd],
                               xor_ts, below)
        s_ref[hd] = st_new
        z_s[:, sl] = (_rms(o, gn) * _silu(gg_s[:, sl])).astype(BF16)
    out_ref[...] = x + jnp.dot(z_s[...], wo_ref[...], preferred_element_type=F32)

    @pl.when(t % tiles_per_seq == tiles_per_seq - 1)
    def _():
        for hd in range(nh):
            st_ref[hd] = s_ref[hd].T


def hgrn_prompt(x, g, w_in, lb_raw, g_norm, w_out, *, layer, seq):
    m, d = x.shape
    nh = d // HEAD_DIM
    tm = min(_gcb_tile(seq), 256)
    tps = seq // tm
    return pl.pallas_call(
        functools.partial(_hgrn_prompt_kernel, layer=layer, tiles_per_seq=tps),
        grid=(m // tm,),
        in_specs=[pl.BlockSpec((tm, d), lambda i: (i, 0)), _const_spec((1, d)), _layer_spec(*w_in),
                  _const_spec(lb_raw.shape), _const_spec((1, HEAD_DIM)), _layer_spec(*w_out)],
        out_specs=[pl.BlockSpec((tm, d), lambda i: (i, 0)),
                   pl.BlockSpec((None, nh, HEAD_DIM, HEAD_DIM), lambda i: (i // tps, 0, 0, 0))],
        out_shape=[jax.ShapeDtypeStruct((m, d), F32),
                   jax.ShapeDtypeStruct((m // seq, nh, HEAD_DIM, HEAD_DIM), F32)],
        scratch_shapes=[pltpu.VMEM((nh, HEAD_DIM, HEAD_DIM), F32)] + [pltpu.VMEM((tm, d), F32)] * 6
        + [pltpu.VMEM((tm, d), BF16)],
        compiler_params=_params("arbitrary"),
    )(x, g.reshape(1, d), w_in[0], lb_raw, g_norm.reshape(1, HEAD_DIM), w_out[0])


def _column_replicate(row, n_cols):
    eye = lax.broadcasted_iota(jnp.int32, (HEAD_DIM, HEAD_DIM), 0) == lax.broadcasted_iota(
        jnp.int32, (HEAD_DIM, HEAD_DIM), 1)
    hi = jnp.where(eye, row, 0.0).astype(BF16)
    lo = jnp.where(eye, row - row.astype(BF16).astype(F32), 0.0).astype(BF16)
    rep = jnp.dot(jnp.concatenate([hi, lo], axis=0), jnp.ones((HEAD_DIM, n_cols), BF16), preferred_element_type=F32)
    return rep[:HEAD_DIM] + rep[HEAD_DIM:]


def _hgrn_decode_layer_kernel(x_ref, g_ref, wq_ref, wf_ref, wi_ref, wg_ref, s_ref, lb_ref, gn_ref, wo_ref,
                              out_ref, sn_ref, o_s, *, layer):
    bd = x_ref.shape[0]
    hd = pl.program_id(0)
    x = x_ref[...]
    hb = _rms(x, g_ref[...]).astype(BF16)
    lower_row = _lower_bound([lb_ref[pl.ds(j, 1), :] for j in range(lb_ref.shape[0])], layer)
    lower = _column_replicate(lower_row, bd)
    f_t, k_t = _hgrn_gates(lax.dot_general(wf_ref[...], hb, NT_DIMS, preferred_element_type=F32), lower)
    q_t = _silu(lax.dot_general(wq_ref[...], hb, NT_DIMS, preferred_element_type=F32))
    v = jnp.dot(hb, wi_ref[...], preferred_element_type=F32)
    for b in range(bd):
        col = slice(b, b + 1)
        s_new = (jnp.broadcast_to(f_t[:, col], (HEAD_DIM, HEAD_DIM)) * s_ref[b]
                 + jnp.broadcast_to(k_t[:, col], (HEAD_DIM, HEAD_DIM)) * v[col, :])
        sn_ref[b] = s_new
        o_s[col, :] = jnp.sum(jnp.broadcast_to(q_t[:, col], (HEAD_DIM, HEAD_DIM)) * s_new, axis=0, keepdims=True)
    z = _rms(o_s[...], gn_ref[...]) * _silu(jnp.dot(hb, wg_ref[...], preferred_element_type=F32))

    @pl.when(hd == 0)
    def _():
        out_ref[...] = x

    out_ref[...] += _bdot(z, wo_ref[...])


def hgrn_decode_layer(x, g, w_in, w_in_t, w_out4, slot, states, lb_raw3, g_norm, *, layer):
    bd, d = x.shape
    nh = d // HEAD_DIM
    depth = lb_raw3.shape[0]

    def head_cols(group):
        return pl.BlockSpec((None, d, HEAD_DIM), lambda h: (slot, 0, group * nh + h))

    def head_rows(group):
        return pl.BlockSpec((None, HEAD_DIM, d), lambda h: (slot, group * nh + h, 0))

    state_spec = pl.BlockSpec((bd, None, HEAD_DIM, HEAD_DIM), lambda h: (0, h, 0, 0))
    return pl.pallas_call(
        functools.partial(_hgrn_decode_layer_kernel, layer=layer),
        grid=(nh,),
        in_specs=[_const_spec((bd, d)), _const_spec((1, d)), head_rows(0), head_rows(1), head_cols(2), head_cols(3),
                  pl.BlockSpec((None, bd, None, HEAD_DIM, HEAD_DIM), lambda h: (slot, 0, h, 0, 0)),
                  pl.BlockSpec((None, depth, HEAD_DIM), lambda h: (h, 0, 0)), _const_spec((1, HEAD_DIM)),
                  pl.BlockSpec((None, None, HEAD_DIM, d), lambda h: (slot, h, 0, 0))],
        out_specs=[pl.BlockSpec((bd, d), lambda h: (0, 0)), state_spec],
        out_shape=[jax.ShapeDtypeStruct((bd, d), F32), jax.ShapeDtypeStruct(states.shape[1:], F32)],
        scratch_shapes=[pltpu.VMEM((bd, HEAD_DIM), F32)],
        compiler_params=_params("arbitrary"),
    )(x, g.reshape(1, d), w_in_t, w_in_t, w_in, w_in, states, lb_raw3.transpose(1, 0, 2),
      g_norm.reshape(1, HEAD_DIM), w_out4)


def _rope_tables(pos):
    half = HEAD_DIM // 2
    inv = jnp.exp(-math.log(ROPE_THETA) * jnp.arange(half, dtype=F32) / half)
    ang = pos.astype(F32)[:, None] * inv[None, :]
    cos, sin = jnp.cos(ang), jnp.sin(ang)
    return jnp.concatenate([cos, cos], axis=-1), jnp.concatenate([-sin, sin], axis=-1)


def _qkv_kernel(x_ref, g_ref, w_ref, qn_ref, kn_ref, cos_ref, sin_ref, q_ref, k_ref, v_ref, km_ref):
    d = x_ref.shape[1]
    hb = _rms(x_ref[...], g_ref[...]).astype(BF16)
    cos, sin = cos_ref[...], sin_ref[...]

    def norm_rope(y, gain):
        y = _rms(y, gain)
        return y * cos + pltpu.roll(y, HEAD_DIM // 2, axis=1) * sin

    v_ref[...] = jnp.dot(hb, w_ref[:, pl.ds(2 * d, d)], preferred_element_type=F32)
    for j, (out_ref, gain_ref) in enumerate(((q_ref, qn_ref), (k_ref, kn_ref))):
        y = jnp.dot(hb, w_ref[:, pl.ds(j * d, d)], preferred_element_type=F32)
        for hd in range(d // HEAD_DIM):
            sl = pl.ds(hd * HEAD_DIM, HEAD_DIM)
            yh = norm_rope(y[:, hd * HEAD_DIM:(hd + 1) * HEAD_DIM], gain_ref[...])
            out_ref[:, sl] = yh
            if j == 1:
                km_ref[:, sl] = jnp.mean(yh, axis=0, keepdims=True)


def moba_qkv(x, g, w, qn, kn, cosf, sinf, *, tm):
    m, d = x.shape
    tps = cosf.shape[0] // tm
    return pl.pallas_call(
        _qkv_kernel,
        grid=(m // tm,),
        in_specs=[pl.BlockSpec((tm, d), lambda i: (i, 0)), _const_spec((1, d)), _layer_spec(*w),
                  _const_spec((1, HEAD_DIM)), _const_spec((1, HEAD_DIM)),
                  pl.BlockSpec((tm, HEAD_DIM), lambda i: (i % tps, 0)),
                  pl.BlockSpec((tm, HEAD_DIM), lambda i: (i % tps, 0))],
        out_specs=[pl.BlockSpec((tm, d), lambda i: (i, 0))] * 3 + [pl.BlockSpec((None, 1, d), lambda i: (i, 0, 0))],
        out_shape=[jax.ShapeDtypeStruct((m, d), F32)] * 3 + [jax.ShapeDtypeStruct((m // tm, 1, d), F32)],
        compiler_params=_params("arbitrary"),
    )(x, g.reshape(1, d), w[0], qn.reshape(1, HEAD_DIM), kn.reshape(1, HEAD_DIM), cosf, sinf)


def _top_blocks(gate, n_idx, n_valid, axis):
    neg = -jnp.inf
    g = jnp.where(n_idx < n_valid, gate, neg)
    sel = jnp.zeros(gate.shape, F32)
    big = jnp.int32(2 ** 30)
    for _ in range(MB_TOPK):
        m = jnp.max(g, axis=axis, keepdims=True)
        idx = jnp.min(jnp.where(g == m, n_idx, big), axis=axis, keepdims=True)
        hit = n_idx == idx
        sel = jnp.where(hit, jnp.where(m > neg, 1.0, sel), sel)
        g = jnp.where(hit, neg, g)
    return sel


def _moba_prompt_kernel(q_ref, k_ref, v_ref, km_ref, o_ref, kb_ref, vt_ref, *, scale):
    nblk = km_ref.shape[0]
    blk = q_ref.shape[0] // nblk
    for n in range(nblk):
        rows = pl.ds(n * blk, blk)
        kb_ref[n] = k_ref[rows, :].astype(BF16)
        vt_ref[:, rows] = v_ref[rows, :].T.astype(BF16)
    km = km_ref[...]
    n_idx = lax.broadcasted_iota(jnp.int32, (nblk, blk), 0)
    causal = lax.broadcasted_iota(jnp.int32, (blk, blk), 0) <= lax.broadcasted_iota(jnp.int32, (blk, blk), 1)
    for i in range(nblk):
        q = q_ref[pl.ds(i * blk, blk), :]
        qb = (q * scale).astype(BF16)

        def scores(n):
            return lax.dot_general(kb_ref[n], qb, NT_DIMS, preferred_element_type=F32)

        s = [None] * i + [jnp.where(causal, scores(i), -jnp.inf)]
        if i > 0:
            gate = lax.dot_general(km, q, NT_DIMS, precision=lax.Precision.HIGHEST, preferred_element_type=F32)
            sel = _top_blocks(gate, n_idx, i, 0)
            for n in range(i):
                s[n] = jnp.where(sel[n:n + 1, :] > 0.5, scores(n), -jnp.inf)
        m = functools.reduce(jnp.maximum, [jnp.max(sn, axis=0, keepdims=True) for sn in s])
        p = [jnp.exp2(sn - m) for sn in s]
        l = functools.reduce(jnp.add, [jnp.sum(pn, axis=0, keepdims=True) for pn in p])
        pcat = jnp.concatenate([pn.astype(BF16) for pn in p], axis=0)
        acc = jnp.dot(vt_ref[:, pl.ds(0, (i + 1) * blk)], pcat, preferred_element_type=F32)
        o_ref[pl.ds(i * blk, blk), :] = (acc / l).T


def moba_prompt_attention(q, k, v, kmeans, *, batch, seq):
    m, d = q.shape
    nh = d // HEAD_DIM
    nblk = seq // MB_BLOCK
    return pl.pallas_call(
        functools.partial(_moba_prompt_kernel, scale=HEAD_DIM ** -0.5 * LOG2_E),
        grid=(batch, nh),
        in_specs=[pl.BlockSpec((seq, HEAD_DIM), lambda b, h: (b, h))] * 3
        + [pl.BlockSpec((None, nblk, HEAD_DIM), lambda b, h: (b, 0, h))],
        out_specs=pl.BlockSpec((seq, HEAD_DIM), lambda b, h: (b, h)),
        out_shape=jax.ShapeDtypeStruct((m, d), F32),
        scratch_shapes=[pltpu.VMEM((nblk, MB_BLOCK, HEAD_DIM), BF16), pltpu.VMEM((HEAD_DIM, seq), BF16)],
        compiler_params=_params("arbitrary", "arbitrary"),
    )(q, k, v, kmeans)


def _top_block_ids(gate, n_idx, n_take):
    g = gate
    ids = []
    for _ in range(n_take):
        m = jnp.max(g, axis=1, keepdims=True)
        idx = jnp.min(jnp.where(g == m, n_idx, jnp.int32(2 ** 30)), axis=1, keepdims=True)
        ids.append(idx)
        g = jnp.where(n_idx == idx, -jnp.inf, g)
    return ids


def _moba_decode_kernel(pt_ref, q_ref, kn_ref, vn_ref, *rest, scale, n_pages, slot, n_samples):
    k_refs = rest[:n_pages]
    cv_hbm, o_ref, s_ref, vbuf, stat_ref, ids_ref, sems = rest[n_pages:]
    nh = q_ref.shape[0]
    nblk = n_pages // 2
    nsel = min(MB_TOPK, nblk)
    page = k_refs[0].shape[0] // nh
    t = pl.program_id(0)
    par = t % 2

    def v_copy(p, b, h, r, e):
        n = ids_ref[p, h * nsel + r]
        return pltpu.make_async_copy(cv_hbm.at[slot, pt_ref[b, 2 * n + e], :, h, :],
                                     vbuf.at[p, h, r, pl.ds(e * page, page), :], sems.at[p])

    def v_copies(p, b):
        return [v_copy(p, b, h, r, e) for h in range(nh) for r in range(nsel) for e in range(2)]

    @pl.when(t < n_samples)
    def _():
        q = q_ref[...]
        lane = lax.broadcasted_iota(jnp.int32, (nh, HEAD_DIM), 1)
        ones = jnp.ones((HEAD_DIM, HEAD_DIM), BF16)
        gates = jnp.full((nh, HEAD_DIM), -jnp.inf, F32)
        mx = []
        for n in range(nblk):
            ksum, mxn = [], []
            for h in range(nh):
                qh = q[h:h + 1, :] * scale
                ks = jnp.zeros((1, HEAD_DIM), F32)
                mh = jnp.full((1, HEAD_DIM), -jnp.inf, F32)
                for e in range(2):
                    kh = k_refs[2 * n + e][pl.ds(h, page, stride=nh), :]
                    sc = jnp.dot((kh * qh).astype(BF16), ones, preferred_element_type=F32)
                    s_ref[par, n, h, pl.ds(e * page, page), :] = sc
                    ks = ks + jnp.sum(kh, axis=0, keepdims=True)
                    mh = jnp.maximum(mh, jnp.max(sc, axis=0, keepdims=True))
                ksum.append(ks)
                mxn.append(mh)
            mx.append(jnp.concatenate(mxn, axis=0))
            mean = jnp.concatenate(ksum, axis=0) / (2 * page)
            gates = jnp.where(lane == n, jnp.sum(q * mean, axis=1, keepdims=True), gates)
        ids = _top_block_ids(gates, lane, nsel)
        s_own = jnp.sum(q * kn_ref[...], axis=1, keepdims=True) * scale
        m = jnp.broadcast_to(s_own, (nh, HEAD_DIM))
        ids_b = [jnp.broadcast_to(idx, (nh, HEAD_DIM)) for idx in ids]
        for n in range(nblk):
            mx_n = -jnp.inf
            for idx in ids_b:
                mx_n = jnp.where(idx == n, mx[n], mx_n)
            m = jnp.maximum(m, mx_n)
        p_own = jnp.exp2(s_own - m)
        stat_ref[par, 0] = m
        stat_ref[par, 1] = p_own
        stat_ref[par, 2] = p_own * vn_ref[...]
        for r in range(nsel):
            for h in range(nh):
                ids_ref[par, h * nsel + r] = ids[r][h, 0]
        for cp in v_copies(par, t):
            cp.start()

    @pl.when(t >= 1)
    def _():
        prev = 1 - par
        for cp in v_copies(prev, t - 1):
            cp.wait()
        m = stat_ref[prev, 0]
        l_rows, acc_rows = [], []
        for h in range(nh):
            l = jnp.zeros((1, HEAD_DIM), F32)
            acc = jnp.zeros((1, HEAD_DIM), F32)
            for r in range(nsel):
                p = jnp.exp2(s_ref[prev, ids_ref[prev, h * nsel + r], h] - m[h:h + 1, :])
                l = l + jnp.sum(p, axis=0, keepdims=True)
                acc = acc + jnp.sum(p * vbuf[prev, h, r], axis=0, keepdims=True)
            l_rows.append(l)
            acc_rows.append(acc)
        l = stat_ref[prev, 1] + jnp.concatenate(l_rows, axis=0)
        o_ref[...] = (stat_ref[prev, 2] + jnp.concatenate(acc_rows, axis=0)) / l


def moba_decode_attention(q, k_new, v_new, cache_k, cache_v, slot, page_table):
    bd, nh, _ = q.shape
    n_layers, n_phys, page = cache_k.shape[:3]
    n_pages = page_table.shape[1]
    assert 2 * page == MB_BLOCK and n_pages % 2 == 0
    nblk = n_pages // 2
    nsel = min(MB_TOPK, nblk)
    page_rows = page * nh
    ck = cache_k.reshape(n_layers, n_phys, page_rows, HEAD_DIM)

    def kmap(e):
        return lambda t, pt: (slot, pt[jnp.minimum(t, bd - 1), e], 0, 0)

    page_block = (None, None, page_rows, HEAD_DIM)
    vec_k = pl.BlockSpec((None, nh, HEAD_DIM), lambda t, pt: (jnp.minimum(t, bd - 1), 0, 0))
    grid_spec = pltpu.PrefetchScalarGridSpec(
        num_scalar_prefetch=1,
        grid=(bd + 1,),
        in_specs=[vec_k, vec_k, vec_k] + [pl.BlockSpec(page_block, kmap(e)) for e in range(n_pages)]
        + [pl.BlockSpec(memory_space=pl.ANY)],
        out_specs=pl.BlockSpec((None, nh, HEAD_DIM), lambda t, pt: (jnp.maximum(t - 1, 0), 0, 0)),
        scratch_shapes=[pltpu.VMEM((2, nblk, nh, MB_BLOCK, HEAD_DIM), F32),
                        pltpu.VMEM((2, nh, nsel, MB_BLOCK, HEAD_DIM), F32),
                        pltpu.VMEM((2, 3, nh, HEAD_DIM), F32),
                        pltpu.SMEM((2, nh * nsel), jnp.int32),
                        pltpu.SemaphoreType.DMA((2,))],
    )
    return pl.pallas_call(
        functools.partial(_moba_decode_kernel, scale=HEAD_DIM ** -0.5 * LOG2_E, n_pages=n_pages, slot=slot,
                          n_samples=bd),
        grid_spec=grid_spec,
        out_shape=jax.ShapeDtypeStruct((bd, nh, HEAD_DIM), F32),
        compiler_params=_params("arbitrary"),
    )(page_table, q, k_new, v_new, *([ck] * n_pages), cache_v)


def kernel(x_prompt, x_sample, state_shortconv, state_hgrn, cache_k, cache_v, page_table, state_ffn_conv, norm_mix, norm_ffn, w_in_a, w_conv_a, w_out_a, w_in_b, lb_raw, g_norm_b, w_out_b, w_qkv_c, q_norm_c, k_norm_c, w_out_c, w_up, w_ffn_conv, b_ffn_conv, w_down):
    bp, seq, d = x_prompt.shape
    bd = x_sample.shape[0]
    assert x_sample.shape[1] == 1
    depth = norm_mix.shape[0]
    nh = d // HEAD_DIM
    past_len = page_table.shape[1] * cache_k.shape[2]
    assert past_len % MB_BLOCK == 0 and seq % MB_BLOCK == 0

    xp = x_prompt.reshape(bp * seq, d)
    xs = x_sample.reshape(bd, d)
    lb_raw3 = lb_raw.reshape(depth, nh, HEAD_DIM)
    w_in_a, w_out_a, w_in_b, w_out_b, w_qkv_c, w_out_c, w_up, w_down = (
        w.astype(BF16) for w in (w_in_a, w_out_a, w_in_b, w_out_b, w_qkv_c, w_out_c, w_up, w_down))
    w_in_b_t = w_in_b[:, :, :2 * d].transpose(0, 2, 1)
    w_out_b4 = w_out_b.reshape(-1, nh, HEAD_DIM, d)
    outs = {name: [] for name in ("sc_p", "sc_s", "hg_p", "hg_s", "kp", "vp", "ks", "vs", "fc_p", "fc_s")}
    n_conv = n_hgrn = n_moba = 0
    for i in range(depth):
        mixer = i % 3
        pre = None
        if mixer == 0:
            j, n_conv = n_conv, n_conv + 1
            w1, w2 = (w_in_a, j), (w_out_a, j)
            xp, st_p, xs, st_s = gated_conv_block(xp, xs, state_shortconv[j], norm_mix[i], w1, w_conv_a[j], None,
                                                  w2, seq=seq)
            outs["sc_p"].append(st_p)
            outs["sc_s"].append(st_s)
        elif mixer == 1:
            j, n_hgrn = n_hgrn, n_hgrn + 1
            xp, st_p = hgrn_prompt(xp, norm_mix[i], (w_in_b, j), lb_raw, g_norm_b[j], (w_out_b, j), layer=i, seq=seq)
            xs, st_s = hgrn_decode_layer(xs, norm_mix[i], w_in_b, w_in_b_t, w_out_b4, j, state_hgrn, lb_raw3,
                                         g_norm_b[j], layer=i)
            outs["hg_p"].append(st_p)
            outs["hg_s"].append(st_s)
        else:
            j, n_moba = n_moba, n_moba + 1
            wq, wo = (w_qkv_c, j), (w_out_c, j)
            cos_p, sin_p = _rope_tables(jnp.arange(seq, dtype=jnp.int32))
            q, k, v, km = moba_qkv(xp, norm_mix[i], wq, q_norm_c[j], k_norm_c[j], cos_p, sin_p, tm=MB_BLOCK)
            o = moba_prompt_attention(q, k, v, km.reshape(bp, seq // MB_BLOCK, d), batch=bp, seq=seq)
            cos_s, sin_s = _rope_tables(jnp.full((bd,), past_len, jnp.int32))
            qs, ks, vs, _ = moba_qkv(xs, norm_mix[i], wq, q_norm_c[j], k_norm_c[j], cos_s, sin_s, tm=bd)
            os_ = moba_decode_attention(qs.reshape(bd, nh, HEAD_DIM), ks.reshape(bd, nh, HEAD_DIM),
                                        vs.reshape(bd, nh, HEAD_DIM), cache_k, cache_v, j, page_table)
            pre = (o, os_.reshape(bd, d), wo)
            outs["kp"].append(k.reshape(bp, seq, nh, HEAD_DIM))
            outs["vp"].append(v.reshape(bp, seq, nh, HEAD_DIM))
            outs["ks"].append(ks.reshape(bd, 1, nh, HEAD_DIM))
            outs["vs"].append(vs.reshape(bd, 1, nh, HEAD_DIM))
        w1, w2 = (w_up, i), (w_down, i)
        xp, st_p, xs, st_s = gated_conv_block(xp, xs, state_ffn_conv[i], norm_ffn[i], w1, w_ffn_conv[i],
                                              b_ffn_conv[i], w2, seq=seq, pre=pre)
        outs["fc_p"].append(st_p)
        outs["fc_s"].append(st_s)
    return (xp.reshape(bp, seq, d), xs.reshape(bd, 1, d), jnp.stack(outs["sc_p"]), jnp.stack(outs["sc_s"]),
            jnp.stack(outs["hg_p"]), jnp.stack(outs["hg_s"]), jnp.stack(outs["kp"]), jnp.stack(outs["vp"]),
            jnp.stack(outs["ks"]), jnp.stack(outs["vs"]), jnp.stack(outs["fc_p"]), jnp.stack(outs["fc_s"]))
```

```python
import functools
import math

import jax
import jax.numpy as jnp
from jax import lax
from jax.experimental import pallas as pl
from jax.experimental.pallas import tpu as pltpu

F32 = jnp.float32
BF16 = jnp.bfloat16

EPS = 1e-6
HEAD_DIM = 128
CONV_TAPS = 3
HG_CHUNK = 64
HG_SUB = 16
MB_BLOCK = 256
MB_TOPK = 3
ROPE_THETA = 10000.0
LOG2_E = math.log2(math.e)
V7X_VMEM_BYTES = 64 * 1024 * 1024
VMEM_LIMIT = V7X_VMEM_BYTES - 8 * 1024 * 1024

NT_DIMS = (((1,), (1,)), ((), ()))
TN_DIMS = (((0,), (0,)), ((), ()))


def _params(*sem):
    return pltpu.CompilerParams(dimension_semantics=sem, vmem_limit_bytes=VMEM_LIMIT)


def _const_spec(shape):
    n = len(shape)
    return pl.BlockSpec(shape, lambda *_: (0,) * n, pipeline_mode=pl.Buffered(1))


def _layer_spec(stack, layer):
    shape = stack.shape[1:]
    return pl.BlockSpec((None,) + shape, lambda *_: (layer,) + (0,) * len(shape), pipeline_mode=pl.Buffered(1))


def _rms(x, g):
    return x * lax.rsqrt(jnp.mean(x * x, axis=-1, keepdims=True) + EPS) * g


def _sigmoid(x):
    return 1.0 / (1.0 + jnp.exp(-x))


def _silu(x):
    return x * _sigmoid(x)


def _bdot(a, b):
    return jnp.dot(a.astype(BF16), b.astype(BF16), preferred_element_type=F32)


def _bdot_nt(a, b):
    return lax.dot_general(a.astype(BF16), b.astype(BF16), NT_DIMS, preferred_element_type=F32)


def _gcb_rows(x, taps, g_ref, w1_ref, wc_ref, bc_ref, w2_ref, z_ref, *, width, cw):
    ffn = bc_ref is not None
    h = _rms(x, g_ref[...]).astype(BF16)
    for c in range(width // cw):
        sl = pl.ds(c * cw, cw)

        def proj(j):
            return jnp.dot(h, w1_ref[:, pl.ds(j * width + c * cw, cw)], preferred_element_type=F32)

        if ffn:
            u, gate = proj(0), proj(1)
        else:
            bg = proj(0)
            u = proj(1) * proj(2)
        um2, um1 = taps(sl, u)
        wc = wc_ref[:, sl]
        conv = wc[0:1] * um2 + wc[1:2] * um1 + wc[2:3] * u
        z = _silu(conv + bc_ref[:, sl]) * gate if ffn else bg * conv
        z_ref[:, sl] = z.astype(BF16)
    return x + jnp.dot(z_ref[...], w2_ref[...], preferred_element_type=F32)


def _gcb_kernel(*refs, ffn, pre, n_tiles, tiles_per_seq, width, cw):
    refs = list(refs)
    o_ref, os_ref, wo_ref = (refs.pop(0), refs.pop(0), refs.pop(0)) if pre else (None, None, None)
    x_ref, xs_ref, sm2_ref, sm1_ref, g_ref, w1_ref, wc_ref = refs[:7]
    refs = refs[7:]
    bc_ref = refs.pop(0) if ffn else None
    w2_ref, out_ref, st_ref, outs_ref, unew_ref, z_ref, zs_ref, carry_ref = refs
    step = pl.program_id(0)
    weights = (g_ref, w1_ref, wc_ref, bc_ref, w2_ref)

    def with_pre(x, o):
        return x + _bdot(o[...], wo_ref[...]) if pre else x

    @pl.when(step < n_tiles)
    def _():
        tm = x_ref.shape[0]

        @pl.when(step % tiles_per_seq == 0)
        def _():
            carry_ref[...] = jnp.zeros_like(carry_ref)

        row = lax.broadcasted_iota(jnp.int32, (tm, cw), 0)

        def taps(sl, u):
            c0, c1 = carry_ref[0:1, sl], carry_ref[1:2, sl]
            um1 = jnp.where(row == 0, c1, pltpu.roll(u, 1, axis=0))
            um2 = jnp.where(row == 0, c0, jnp.where(row == 1, c1, pltpu.roll(u, 2, axis=0)))
            carry_ref[0:2, sl] = u[tm - 2:tm, :]
            return um2, um1

        out_ref[...] = _gcb_rows(with_pre(x_ref[...], o_ref), taps, *weights, z_ref, width=width, cw=cw)
        st_ref[...] = carry_ref[0:2, :]

    @pl.when(step == n_tiles)
    def _():
        def taps(sl, u):
            unew_ref[:, sl] = u
            return sm2_ref[:, sl], sm1_ref[:, sl]

        outs_ref[...] = _gcb_rows(with_pre(xs_ref[...], os_ref), taps, *weights, zs_ref, width=width, cw=cw)


def _gcb_tile(rows):
    for tm in (512, 256, 128, 64, 32, 16, 8):
        if rows % tm == 0:
            return tm
    raise ValueError(f"row count {rows} must be a multiple of 8")


def gated_conv_block(x, xs, state, g, w1, wc, bc, w2, *, seq, pre=None):
    m, d = x.shape
    bd = xs.shape[0]
    width = w2[0].shape[1]
    ffn = bc is not None
    tm = _gcb_tile(seq)
    tps = seq // tm
    n_tiles = m // tm
    cw = 256

    def tile(i):
        return jnp.minimum(i, n_tiles - 1)

    args = [x, xs, state[:, 0], state[:, 1], g.reshape(1, d), w1[0], wc] + (
        [bc.reshape(1, width)] if ffn else []) + [w2[0]]
    in_specs = [pl.BlockSpec((tm, d), lambda i: (tile(i), 0)), _const_spec((bd, d)), _const_spec((bd, width)),
                _const_spec((bd, width)), _const_spec((1, d)), _layer_spec(*w1), _const_spec(wc.shape)] + (
        [_const_spec((1, width))] if ffn else []) + [_layer_spec(*w2)]
    if pre:
        o, os_, wo = pre
        args = [o, os_, wo[0]] + args
        in_specs = [pl.BlockSpec((tm, o.shape[1]), lambda i: (tile(i), 0)), _const_spec(os_.shape),
                    _layer_spec(*wo)] + in_specs
    out, st, outs, unew = pl.pallas_call(
        functools.partial(_gcb_kernel, ffn=ffn, pre=bool(pre), n_tiles=n_tiles, tiles_per_seq=tps, width=width,
                          cw=cw),
        grid=(n_tiles + 1,),
        in_specs=in_specs,
        out_specs=[pl.BlockSpec((tm, d), lambda i: (tile(i), 0)),
                   pl.BlockSpec((None, CONV_TAPS - 1, width), lambda i: (tile(i) // tps, 0, 0)),
                   pl.BlockSpec((bd, d), lambda i: (0, 0)), pl.BlockSpec((bd, width), lambda i: (0, 0))],
        out_shape=[jax.ShapeDtypeStruct((m, d), F32), jax.ShapeDtypeStruct((m // seq, CONV_TAPS - 1, width), F32),
                   jax.ShapeDtypeStruct((bd, d), F32), jax.ShapeDtypeStruct((bd, width), F32)],
        scratch_shapes=[pltpu.VMEM((tm, width), BF16), pltpu.VMEM((bd, width), BF16), pltpu.VMEM((8, width), F32)],
        compiler_params=_params("arbitrary"),
    )(*args)
    return out, st, outs, jnp.stack([state[:, 1], unew], axis=1)


def _lower_bound(rows, layer):
    mx = functools.reduce(jnp.maximum, rows)
    e = [jnp.exp(r - mx) for r in rows]
    tot = functools.reduce(jnp.add, e)
    w = [a / tot for a in e]
    cum = w[0]
    for j in range(1, layer + 1):
        cum = cum + w[j]
    return cum - w[0]


def _split3(x):
    hi = x.astype(BF16)
    r = x - hi.astype(F32)
    mid = r.astype(BF16)
    lo = (r - mid.astype(F32)).astype(BF16)
    return hi, mid, lo


def _level_decay(cum, f, b):
    t = cum.shape[0]
    if b >= 4:
        c3 = cum.reshape(t // (2 * b), 2 * b, HEAD_DIM)
        d = cum - jnp.broadcast_to(c3[:, b - 1:b, :], c3.shape).reshape(t, HEAD_DIM)
        return jnp.exp2(-jnp.abs(d))
    r = lax.broadcasted_iota(jnp.int32, cum.shape, 0) & (2 * b - 1)
    if b == 1:
        return jnp.where(r == 1, f, 1.0)
    return jnp.where(r == 0, pltpu.roll(f, t - 1, axis=0),
                     jnp.where(r == 1, 1.0, jnp.where(r == 2, f, pltpu.roll(f, 1, axis=0) * f)))


def _hgrn_masks(t):
    if t == 2 * HEAD_DIM:
        row = lax.broadcasted_iota(jnp.int32, (t // 2, t), 0)
        col = lax.broadcasted_iota(jnp.int32, (t // 2, t), 1) & (t // 2 - 1)
    else:
        row = lax.broadcasted_iota(jnp.int32, (t, t), 0)
        col = lax.broadcasted_iota(jnp.int32, (t, t), 1)
    return row ^ col, row > col


def _pack_halves(y):
    half = y.shape[0] // 2
    zeros = jnp.zeros((half, HEAD_DIM), y.dtype)
    return jnp.concatenate([jnp.concatenate([y[:half], zeros], axis=1),
                            jnp.concatenate([zeros, y[half:]], axis=1)], axis=0)


def _hgrn_tile(q, k, v, f, cum, st, xor_ts, below):
    t = q.shape[0]
    half = t // 2
    packed = t == 2 * HEAD_DIM
    o = _bdot_nt(q * jnp.exp2(cum), st) + jnp.sum(q * k, axis=-1, keepdims=True) * v
    vb = v.astype(BF16)

    def operands(b):
        x = _level_decay(cum, f, b)
        return (q * x).astype(BF16), (k * x).astype(BF16)

    a = None
    b = half // 2 if packed else half
    while b >= 1:
        yq, yk = operands(b)
        if packed:
            al = lax.dot_general(jnp.concatenate([yq[:half], yq[half:]], axis=1), _pack_halves(yk), NT_DIMS,
                                 preferred_element_type=F32)
        else:
            al = lax.dot_general(yq, yk, NT_DIMS, preferred_element_type=F32)
        a = al if a is None else jnp.where(xor_ts < 2 * b, al, a)
        b //= 2
    a = jnp.where(below, a, 0.0).astype(BF16)
    if packed:
        yq, yk = operands(half)
        cross = lax.dot_general(yq[half:], yk[:half], NT_DIMS, preferred_element_type=F32)
        o_p = jnp.dot(a, _pack_halves(vb), preferred_element_type=F32)
        o = o + jnp.concatenate([o_p[:, :HEAD_DIM], o_p[:, HEAD_DIM:] + _bdot(cross, vb[:half])], axis=0)
    else:
        o = o + jnp.dot(a, vb, preferred_element_type=F32)
    last = cum[t - 1:t]
    upd = lax.dot_general(v.astype(BF16), (k * jnp.exp2(last - cum)).astype(BF16), TN_DIMS,
                          preferred_element_type=F32)
    return o, st * jnp.exp2(last) + upd


def _hgrn_gates(fz, lower):
    f = lower + (1.0 - lower) * _sigmoid(fz)
    k = (1.0 - lower) * _sigmoid(-fz)
    return f, k


def _hgrn_prompt_kernel(x_ref, g_ref, w_ref, lb_ref, gn_ref, wo_ref, out_ref, st_ref,
                        s_ref, q_s, k_s, v_s, gg_s, f_s, cum_s, z_s, *, layer, tiles_per_seq):
    tm, d = x_ref.shape
    nh = d // HEAD_DIM
    t = pl.program_id(0)

    @pl.when(t % tiles_per_seq == 0)
    def _():
        s_ref[...] = jnp.zeros_like(s_ref)

    def proj(j):
        return jnp.dot(hb, w_ref[:, pl.ds(j * d, d)], preferred_element_type=F32)

    x = x_ref[...]
    hb = _rms(x, g_ref[...]).astype(BF16)
    lower = _lower_bound([lb_ref[j:j + 1, :] for j in range(lb_ref.shape[0])], layer)
    f, kk = _hgrn_gates(proj(1), lower)
    k_s[...] = kk
    f_s[...] = f
    row = lax.broadcasted_iota(jnp.int32, (tm, tm), 0)
    col = lax.broadcasted_iota(jnp.int32, (tm, tm), 1)
    tri = jnp.where(row >= col, 1.0, 0.0).astype(BF16)
    cum_s[...] = functools.reduce(jnp.add, [jnp.dot(tri, part, preferred_element_type=F32)
                                            for part in _split3(jnp.log2(f))])
    q_s[...] = _silu(proj(0))
    v_s[...] = proj(2)
    gg_s[...] = proj(3)
    xor_ts, below = _hgrn_masks(tm)
    gn = gn_ref[...]
    for hd in range(nh):
        sl = pl.ds(hd * HEAD_DIM, HEAD_DIM)
        o, st_new = _hgrn_tile(q_s[:, sl], k_s[:, sl], v_s[:, sl], f_s[:, sl], cum_s[:, sl], s_ref[hd],
                               xor_ts, below)
        s_ref[hd] = st_new
        z_s[:, sl] = (_rms(o, gn) * _silu(gg_s[:, sl])).astype(BF16)
    out_ref[...] = x + jnp.dot(z_s[...], wo_ref[...], preferred_element_type=F32)

    @pl.when(t % tiles_per_seq == tiles_per_seq - 1)
    def _():
        for hd in range(nh):
            st_ref[hd] = s_ref[hd].T


def hgrn_prompt(x, g, w_in, lb_raw, g_norm, w_out, *, layer, seq):
    m, d = x.shape
    nh = d // HEAD_DIM
    tm = min(_gcb_tile(seq), 256)
    tps = seq // tm
    return pl.pallas_call(
        functools.partial(_hgrn_prompt_kernel, layer=layer, tiles_per_seq=tps),
        grid=(m // tm,),
        in_specs=[pl.BlockSpec((tm, d), lambda i: (i, 0)), _const_spec((1, d)), _layer_spec(*w_in),
                  _const_spec(lb_raw.shape), _const_spec((1, HEAD_DIM)), _layer_spec(*w_out)],
        out_specs=[pl.BlockSpec((tm, d), lambda i: (i, 0)),
                   pl.BlockSpec((None, nh, HEAD_DIM, HEAD_DIM), lambda i: (i // tps, 0, 0, 0))],
        out_shape=[jax.ShapeDtypeStruct((m, d), F32),
                   jax.ShapeDtypeStruct((m // seq, nh, HEAD_DIM, HEAD_DIM), F32)],
        scratch_shapes=[pltpu.VMEM((nh, HEAD_DIM, HEAD_DIM), F32)] + [pltpu.VMEM((tm, d), F32)] * 6
        + [pltpu.VMEM((tm, d), BF16)],
        compiler_params=_params("arbitrary"),
    )(x, g.reshape(1, d), w_in[0], lb_raw, g_norm.reshape(1, HEAD_DIM), w_out[0])


def _column_replicate(row, n_cols):
    eye = lax.broadcasted_iota(jnp.int32, (HEAD_DIM, HEAD_DIM), 0) == lax.broadcasted_iota(
        jnp.int32, (HEAD_DIM, HEAD_DIM), 1)
    hi = jnp.where(eye, row, 0.0).astype(BF16)
    lo = jnp.where(eye, row - row.astype(BF16).astype(F32), 0.0).astype(BF16)
    rep = jnp.dot(jnp.concatenate([hi, lo], axis=0), jnp.ones((HEAD_DIM, n_cols), BF16), preferred_element_type=F32)
    return rep[:HEAD_DIM] + rep[HEAD_DIM:]


def _hgrn_decode_layer_kernel(x_ref, g_ref, wq_ref, wf_ref, wi_ref, wg_ref, s_ref, lb_ref, gn_ref, wo_ref,
                              out_ref, sn_ref, o_s, *, layer):
    bd = x_ref.shape[0]
    hd = pl.program_id(0)
    x = x_ref[...]
    hb = _rms(x, g_ref[...]).astype(BF16)
    lower_row = _lower_bound([lb_ref[pl.ds(j, 1), :] for j in range(lb_ref.shape[0])], layer)
    lower = _column_replicate(lower_row, bd)
    f_t, k_t = _hgrn_gates(lax.dot_general(wf_ref[...], hb, NT_DIMS, preferred_element_type=F32), lower)
    q_t = _silu(lax.dot_general(wq_ref[...], hb, NT_DIMS, preferred_element_type=F32))
    v = jnp.dot(hb, wi_ref[...], preferred_element_type=F32)
    for b in range(bd):
        col = slice(b, b + 1)
        s_new = (jnp.broadcast_to(f_t[:, col], (HEAD_DIM, HEAD_DIM)) * s_ref[b]
                 + jnp.broadcast_to(k_t[:, col], (HEAD_DIM, HEAD_DIM)) * v[col, :])
        sn_ref[b] = s_new
        o_s[col, :] = jnp.sum(jnp.broadcast_to(q_t[:, col], (HEAD_DIM, HEAD_DIM)) * s_new, axis=0, keepdims=True)
    z = _rms(o_s[...], gn_ref[...]) * _silu(jnp.dot(hb, wg_ref[...], preferred_element_type=F32))

    @pl.when(hd == 0)
    def _():
        out_ref[...] = x

    out_ref[...] += _bdot(z, wo_ref[...])


def hgrn_decode_layer(x, g, w_in, w_in_t, w_out4, slot, states, lb_raw3, g_norm, *, layer):
    bd, d = x.shape
    nh = d // HEAD_DIM
    depth = lb_raw3.shape[0]

    def head_cols(group):
        return pl.BlockSpec((None, d, HEAD_DIM), lambda h: (slot, 0, group * nh + h))

    def head_rows(group):
        return pl.BlockSpec((None, HEAD_DIM, d), lambda h: (slot, group * nh + h, 0))

    state_spec = pl.BlockSpec((bd, None, HEAD_DIM, HEAD_DIM), lambda h: (0, h, 0, 0))
    return pl.pallas_call(
        functools.partial(_hgrn_decode_layer_kernel, layer=layer),
        grid=(nh,),
        in_specs=[_const_spec((bd, d)), _const_spec((1, d)), head_rows(0), head_rows(1), head_cols(2), head_cols(3),
                  pl.BlockSpec((None, bd, None, HEAD_DIM, HEAD_DIM), lambda h: (slot, 0, h, 0, 0)),
                  pl.BlockSpec((None, depth, HEAD_DIM), lambda h: (h, 0, 0)), _const_spec((1, HEAD_DIM)),
                  pl.BlockSpec((None, None, HEAD_DIM, d), lambda h: (slot, h, 0, 0))],
        out_specs=[pl.BlockSpec((bd, d), lambda h: (0, 0)), state_spec],
        out_shape=[jax.ShapeDtypeStruct((bd, d), F32), jax.ShapeDtypeStruct(states.shape[1:], F32)],
        scratch_shapes=[pltpu.VMEM((bd, HEAD_DIM), F32)],
        compiler_params=_params("arbitrary"),
    )(x, g.reshape(1, d), w_in_t, w_in_t, w_in, w_in, states, lb_raw3.transpose(1, 0, 2),
      g_norm.reshape(1, HEAD_DIM), w_out4)


def _rope_tables(pos):
    half = HEAD_DIM // 2
    inv = jnp.exp(-math.log(ROPE_THETA) * jnp.arange(half, dtype=F32) / half)
    ang = pos.astype(F32)[:, None] * inv[None, :]
    cos, sin = jnp.cos(ang), jnp.sin(ang)
    return jnp.concatenate([cos, cos], axis=-1), jnp.concatenate([-sin, sin], axis=-1)


def _qkv_kernel(x_ref, g_ref, w_ref, qn_ref, kn_ref, cos_ref, sin_ref, q_ref, k_ref, v_ref, km_ref):
    d = x_ref.shape[1]
    hb = _rms(x_ref[...], g_ref[...]).astype(BF16)
    cos, sin = cos_ref[...], sin_ref[...]

    def norm_rope(y, gain):
        y = _rms(y, gain)
        return y * cos + pltpu.roll(y, HEAD_DIM // 2, axis=1) * sin

    v_ref[...] = jnp.dot(hb, w_ref[:, pl.ds(2 * d, d)], preferred_element_type=F32)
    for j, (out_ref, gain_ref) in enumerate(((q_ref, qn_ref), (k_ref, kn_ref))):
        y = jnp.dot(hb, w_ref[:, pl.ds(j * d, d)], preferred_element_type=F32)
        for hd in range(d // HEAD_DIM):
            sl = pl.ds(hd * HEAD_DIM, HEAD_DIM)
            yh = norm_rope(y[:, hd * HEAD_DIM:(hd + 1) * HEAD_DIM], gain_ref[...])
            out_ref[:, sl] = yh
            if j == 1:
                km_ref[:, sl] = jnp.mean(yh, axis=0, keepdims=True)


def moba_qkv(x, g, w, qn, kn, cosf, sinf, *, tm):
    m, d = x.shape
    tps = cosf.shape[0] // tm
    return pl.pallas_call(
        _qkv_kernel,
        grid=(m // tm,),
        in_specs=[pl.BlockSpec((tm, d), lambda i: (i, 0)), _const_spec((1, d)), _layer_spec(*w),
                  _const_spec((1, HEAD_DIM)), _const_spec((1, HEAD_DIM)),
                  pl.BlockSpec((tm, HEAD_DIM), lambda i: (i % tps, 0)),
                  pl.BlockSpec((tm, HEAD_DIM), lambda i: (i % tps, 0))],
        out_specs=[pl.BlockSpec((tm, d), lambda i: (i, 0))] * 3 + [pl.BlockSpec((None, 1, d), lambda i: (i, 0, 0))],
        out_shape=[jax.ShapeDtypeStruct((m, d), F32)] * 3 + [jax.ShapeDtypeStruct((m // tm, 1, d), F32)],
        compiler_params=_params("arbitrary"),
    )(x, g.reshape(1, d), w[0], qn.reshape(1, HEAD_DIM), kn.reshape(1, HEAD_DIM), cosf, sinf)


def _top_blocks(gate, n_idx, n_valid, axis):
    neg = -jnp.inf
    g = jnp.where(n_idx < n_valid, gate, neg)
    sel = jnp.zeros(gate.shape, F32)
    big = jnp.int32(2 ** 30)
    for _ in range(MB_TOPK):
        m = jnp.max(g, axis=axis, keepdims=True)
        idx = jnp.min(jnp.where(g == m, n_idx, big), axis=axis, keepdims=True)
        hit = n_idx == idx
        sel = jnp.where(hit, jnp.where(m > neg, 1.0, sel), sel)
        g = jnp.where(hit, neg, g)
    return sel


def _moba_prompt_kernel(q_ref, k_ref, v_ref, km_ref, o_ref, ko_hbm, vo_hbm, kb_ref, vt_ref, sems, *, scale):
    nblk = km_ref.shape[0]
    blk = q_ref.shape[0] // nblk
    b, hd = pl.program_id(0), pl.program_id(1)
    kv_out = [pltpu.make_async_copy(src, dst.at[b, :, hd, :], sems.at[j])
              for j, (src, dst) in enumerate(((k_ref, ko_hbm), (v_ref, vo_hbm)))]
    for cp in kv_out:
        cp.start()
    for n in range(nblk):
        rows = pl.ds(n * blk, blk)
        kb_ref[n] = k_ref[rows, :].astype(BF16)
        vt_ref[:, rows] = v_ref[rows, :].T.astype(BF16)
    km = km_ref[...]
    n_idx = lax.broadcasted_iota(jnp.int32, (nblk, blk), 0)
    causal = lax.broadcasted_iota(jnp.int32, (blk, blk), 0) <= lax.broadcasted_iota(jnp.int32, (blk, blk), 1)
    for i in range(nblk):
        q = q_ref[pl.ds(i * blk, blk), :]
        qb = (q * scale).astype(BF16)

        def scores(n):
            return lax.dot_general(kb_ref[n], qb, NT_DIMS, preferred_element_type=F32)

        s = [None] * i + [jnp.where(causal, scores(i), -jnp.inf)]
        if i > 0:
            gate = lax.dot_general(km, q, NT_DIMS, precision=lax.Precision.HIGHEST, preferred_element_type=F32)
            sel = _top_blocks(gate, n_idx, i, 0)
            for n in range(i):
                s[n] = jnp.where(sel[n:n + 1, :] > 0.5, scores(n), -jnp.inf)
        m = functools.reduce(jnp.maximum, [jnp.max(sn, axis=0, keepdims=True) for sn in s])
        p = [jnp.exp2(sn - m) for sn in s]
        l = functools.reduce(jnp.add, [jnp.sum(pn, axis=0, keepdims=True) for pn in p])
        pcat = jnp.concatenate([pn.astype(BF16) for pn in p], axis=0)
        acc = jnp.dot(vt_ref[:, pl.ds(0, (i + 1) * blk)], pcat, preferred_element_type=F32)
        o_ref[pl.ds(i * blk, blk), :] = (acc / l).T
    for cp in kv_out:
        cp.wait()


def moba_prompt_attention(q, k, v, kmeans, *, batch, seq):
    m, d = q.shape
    nh = d // HEAD_DIM
    nblk = seq // MB_BLOCK
    kv_shape = jax.ShapeDtypeStruct((batch, seq, nh, HEAD_DIM), F32)
    return pl.pallas_call(
        functools.partial(_moba_prompt_kernel, scale=HEAD_DIM ** -0.5 * LOG2_E),
        grid=(batch, nh),
        in_specs=[pl.BlockSpec((seq, HEAD_DIM), lambda b, h: (b, h))] * 3
        + [pl.BlockSpec((None, nblk, HEAD_DIM), lambda b, h: (b, 0, h))],
        out_specs=[pl.BlockSpec((seq, HEAD_DIM), lambda b, h: (b, h)), pl.BlockSpec(memory_space=pl.ANY),
                   pl.BlockSpec(memory_space=pl.ANY)],
        out_shape=[jax.ShapeDtypeStruct((m, d), F32), kv_shape, kv_shape],
        scratch_shapes=[pltpu.VMEM((nblk, MB_BLOCK, HEAD_DIM), BF16), pltpu.VMEM((HEAD_DIM, seq), BF16),
                        pltpu.SemaphoreType.DMA((2,))],
        compiler_params=_params("arbitrary", "arbitrary"),
    )(q, k, v, kmeans)


def _top_block_ids(gate, n_idx, n_take):
    g = gate
    ids = []
    for _ in range(n_take):
        m = jnp.max(g, axis=1, keepdims=True)
        idx = jnp.min(jnp.where(g == m, n_idx, jnp.int32(2 ** 30)), axis=1, keepdims=True)
        ids.append(idx)
        g = jnp.where(n_idx == idx, -jnp.inf, g)
    return ids


def _moba_decode_kernel(pt_ref, q_ref, kn_ref, vn_ref, *rest, scale, n_pages, slot, n_samples):
    k_refs = rest[:n_pages]
    cv_hbm, o_ref, s_ref, vbuf, stat_ref, ids_ref, sems = rest[n_pages:]
    nh = q_ref.shape[0]
    nblk = n_pages // 2
    nsel = min(MB_TOPK, nblk)
    page = k_refs[0].shape[0] // nh
    t = pl.program_id(0)
    par = t % 2

    def v_copy(p, b, h, r, e):
        n = ids_ref[p, h * nsel + r]
        return pltpu.make_async_copy(cv_hbm.at[slot, pt_ref[b, 2 * n + e], :, h, :],
                                     vbuf.at[p, h, r, pl.ds(e * page, page), :], sems.at[p])

    def v_copies(p, b):
        return [v_copy(p, b, h, r, e) for h in range(nh) for r in range(nsel) for e in range(2)]

    @pl.when(t < n_samples)
    def _():
        q = q_ref[...]
        lane = lax.broadcasted_iota(jnp.int32, (nh, HEAD_DIM), 1)
        ones = jnp.ones((HEAD_DIM, HEAD_DIM), BF16)
        gates = jnp.full((nh, HEAD_DIM), -jnp.inf, F32)
        mx = []
        for n in range(nblk):
            ksum, mxn = [], []
            for h in range(nh):
                qh = q[h:h + 1, :] * scale
                ks = jnp.zeros((1, HEAD_DIM), F32)
                mh = jnp.full((1, HEAD_DIM), -jnp.inf, F32)
                for e in range(2):
                    kh = k_refs[2 * n + e][pl.ds(h, page, stride=nh), :]
                    sc = jnp.dot((kh * qh).astype(BF16), ones, preferred_element_type=F32)
                    s_ref[par, n, h, pl.ds(e * page, page), :] = sc
                    ks = ks + jnp.sum(kh, axis=0, keepdims=True)
                    mh = jnp.maximum(mh, jnp.max(sc, axis=0, keepdims=True))
                ksum.append(ks)
                mxn.append(mh)
            mx.append(jnp.concatenate(mxn, axis=0))
            mean = jnp.concatenate(ksum, axis=0) / (2 * page)
            gates = jnp.where(lane == n, jnp.sum(q * mean, axis=1, keepdims=True), gates)
        ids = _top_block_ids(gates, lane, nsel)
        s_own = jnp.sum(q * kn_ref[...], axis=1, keepdims=True) * scale
        m = jnp.broadcast_to(s_own, (nh, HEAD_DIM))
        ids_b = [jnp.broadcast_to(idx, (nh, HEAD_DIM)) for idx in ids]
        for n in range(nblk):
            mx_n = -jnp.inf
            for idx in ids_b:
                mx_n = jnp.where(idx == n, mx[n], mx_n)
            m = jnp.maximum(m, mx_n)
        p_own = jnp.exp2(s_own - m)
        stat_ref[par, 0] = m
        stat_ref[par, 1] = p_own
        stat_ref[par, 2] = p_own * vn_ref[...]
        for r in range(nsel):
            for h in range(nh):
                ids_ref[par, h * nsel + r] = ids[r][h, 0]
        for cp in v_copies(par, t):
            cp.start()

    @pl.when(t >= 1)
    def _():
        prev = 1 - par
        for cp in v_copies(prev, t - 1):
            cp.wait()
        m = stat_ref[prev, 0]
        l_rows, acc_rows = [], []
        for h in range(nh):
            l = jnp.zeros((1, HEAD_DIM), F32)
            acc = jnp.zeros((1, HEAD_DIM), F32)
            for r in range(nsel):
                p = jnp.exp2(s_ref[prev, ids_ref[prev, h * nsel + r], h] - m[h:h + 1, :])
                l = l + jnp.sum(p, axis=0, keepdims=True)
                acc = acc + jnp.sum(p * vbuf[prev, h, r], axis=0, keepdims=True)
            l_rows.append(l)
            acc_rows.append(acc)
        l = stat_ref[prev, 1] + jnp.concatenate(l_rows, axis=0)
        o_ref[...] = (stat_ref[prev, 2] + jnp.concatenate(acc_rows, axis=0)) / l


def moba_decode_attention(q, k_new, v_new, cache_k, cache_v, slot, page_table):
    bd, nh, _ = q.shape
    n_layers, n_phys, page = cache_k.shape[:3]
    n_pages = page_table.shape[1]
    assert 2 * page == MB_BLOCK and n_pages % 2 == 0
    nblk = n_pages // 2
    nsel = min(MB_TOPK, nblk)
    page_rows = page * nh
    ck = cache_k.reshape(n_layers, n_phys, page_rows, HEAD_DIM)

    def kmap(e):
        return lambda t, pt: (slot, pt[jnp.minimum(t, bd - 1), e], 0, 0)

    page_block = (None, None, page_rows, HEAD_DIM)
    vec_k = pl.BlockSpec((None, nh, HEAD_DIM), lambda t, pt: (jnp.minimum(t, bd - 1), 0, 0))
    grid_spec = pltpu.PrefetchScalarGridSpec(
        num_scalar_prefetch=1,
        grid=(bd + 1,),
        in_specs=[vec_k, vec_k, vec_k] + [pl.BlockSpec(page_block, kmap(e)) for e in range(n_pages)]
        + [pl.BlockSpec(memory_space=pl.ANY)],
        out_specs=pl.BlockSpec((None, nh, HEAD_DIM), lambda t, pt: (jnp.maximum(t - 1, 0), 0, 0)),
        scratch_shapes=[pltpu.VMEM((2, nblk, nh, MB_BLOCK, HEAD_DIM), F32),
                        pltpu.VMEM((2, nh, nsel, MB_BLOCK, HEAD_DIM), F32),
                        pltpu.VMEM((2, 3, nh, HEAD_DIM), F32),
                        pltpu.SMEM((2, nh * nsel), jnp.int32),
                        pltpu.SemaphoreType.DMA((2,))],
    )
    return pl.pallas_call(
        functools.partial(_moba_decode_kernel, scale=HEAD_DIM ** -0.5 * LOG2_E, n_pages=n_pages, slot=slot,
                          n_samples=bd),
        grid_spec=grid_spec,
        out_shape=jax.ShapeDtypeStruct((bd, nh, HEAD_DIM), F32),
        compiler_params=_params("arbitrary"),
    )(page_table, q, k_new, v_new, *([ck] * n_pages), cache_v)


def kernel(x_prompt, x_sample, state_shortconv, state_hgrn, cache_k, cache_v, page_table, state_ffn_conv, norm_mix, norm_ffn, w_in_a, w_conv_a, w_out_a, w_in_b, lb_raw, g_norm_b, w_out_b, w_qkv_c, q_norm_c, k_norm_c, w_out_c, w_up, w_ffn_conv, b_ffn_conv, w_down):
    bp, seq, d = x_prompt.shape
    bd = x_sample.shape[0]
    assert x_sample.shape[1] == 1
    depth = norm_mix.shape[0]
    nh = d // HEAD_DIM
    past_len = page_table.shape[1] * cache_k.shape[2]
    assert past_len % MB_BLOCK == 0 and seq % MB_BLOCK == 0

    xp = x_prompt.reshape(bp * seq, d)
    xs = x_sample.reshape(bd, d)
    lb_raw3 = lb_raw.reshape(depth, nh, HEAD_DIM)
    w_in_a, w_out_a, w_in_b, w_out_b, w_qkv_c, w_out_c, w_up, w_down = (
        w.astype(BF16) for w in (w_in_a, w_out_a, w_in_b, w_out_b, w_qkv_c, w_out_c, w_up, w_down))
    w_in_b_t = w_in_b[:, :, :2 * d].transpose(0, 2, 1)
    w_out_b4 = w_out_b.reshape(-1, nh, HEAD_DIM, d)
    outs = {name: [] for name in ("sc_p", "sc_s", "hg_p", "hg_s", "kp", "vp", "ks", "vs", "fc_p", "fc_s")}
    n_conv = n_hgrn = n_moba = 0
    for i in range(depth):
        mixer = i % 3
        pre = None
        if mixer == 0:
            j, n_conv = n_conv, n_conv + 1
            w1, w2 = (w_in_a, j), (w_out_a, j)
            xp, st_p, xs, st_s = gated_conv_block(xp, xs, state_shortconv[j], norm_mix[i], w1, w_conv_a[j], None,
                                                  w2, seq=seq)
            outs["sc_p"].append(st_p)
            outs["sc_s"].append(st_s)
        elif mixer == 1:
            j, n_hgrn = n_hgrn, n_hgrn + 1
            xp, st_p = hgrn_prompt(xp, norm_mix[i], (w_in_b, j), lb_raw, g_norm_b[j], (w_out_b, j), layer=i, seq=seq)
            xs, st_s = hgrn_decode_layer(xs, norm_mix[i], w_in_b, w_in_b_t, w_out_b4, j, state_hgrn, lb_raw3,
                                         g_norm_b[j], layer=i)
            outs["hg_p"].append(st_p)
            outs["hg_s"].append(st_s)
        else:
            j, n_moba = n_moba, n_moba + 1
            wq, wo = (w_qkv_c, j), (w_out_c, j)
            cos_p, sin_p = _rope_tables(jnp.arange(seq, dtype=jnp.int32))
            q, k, v, km = moba_qkv(xp, norm_mix[i], wq, q_norm_c[j], k_norm_c[j], cos_p, sin_p, tm=MB_BLOCK)
            o, k4, v4 = moba_prompt_attention(q, k, v, km.reshape(bp, seq // MB_BLOCK, d), batch=bp, seq=seq)
            cos_s, sin_s = _rope_tables(jnp.full((bd,), past_len, jnp.int32))
            qs, ks, vs, _ = moba_qkv(xs, norm_mix[i], wq, q_norm_c[j], k_norm_c[j], cos_s, sin_s, tm=bd)
            os_ = moba_decode_attention(qs.reshape(bd, nh, HEAD_DIM), ks.reshape(bd, nh, HEAD_DIM),
                                        vs.reshape(bd, nh, HEAD_DIM), cache_k, cache_v, j, page_table)
            pre = (o, os_.reshape(bd, d), wo)
            outs["kp"].append(k4)
            outs["vp"].append(v4)
            outs["ks"].append(ks.reshape(bd, 1, nh, HEAD_DIM))
            outs["vs"].append(vs.reshape(bd, 1, nh, HEAD_DIM))
        w1, w2 = (w_up, i), (w_down, i)
        xp, st_p, xs, st_s = gated_conv_block(xp, xs, state_ffn_conv[i], norm_ffn[i], w1, w_ffn_conv[i],
                                              b_ffn_conv[i], w2, seq=seq, pre=pre)
        outs["fc_p"].append(st_p)
        outs["fc_s"].append(st_s)
    return (xp.reshape(bp, seq, d), xs.reshape(bd, 1, d), jnp.stack(outs["sc_p"]), jnp.stack(outs["sc_s"]),
            jnp.stack(outs["hg_p"]), jnp.stack(outs["hg_s"]), jnp.stack(outs["kp"]), jnp.stack(outs["vp"]),
            jnp.stack(outs["ks"]), jnp.stack(outs["vs"]), jnp.stack(outs["fc_p"]), jnp.stack(outs["fc_s"]))
```

```python
import functools
import math

import jax
import jax.numpy as jnp
from jax import lax
from jax.experimental import pallas as pl
from jax.experimental.pallas import tpu as pltpu

F32 = jnp.float32
BF16 = jnp.bfloat16

EPS = 1e-6
HEAD_DIM = 128
CONV_TAPS = 3
MB_BLOCK = 256
MB_TOPK = 3
ROPE_THETA = 10000.0
LOG2_E = math.log2(math.e)
V7X_VMEM_BYTES = 64 * 1024 * 1024
VMEM_LIMIT = V7X_VMEM_BYTES - 8 * 1024 * 1024

NT_DIMS = (((1,), (1,)), ((), ()))
TN_DIMS = (((0,), (0,)), ((), ()))


def _params(*sem):
    return pltpu.CompilerParams(dimension_semantics=sem, vmem_limit_bytes=VMEM_LIMIT)


def _const_spec(shape):
    n = len(shape)
    return pl.BlockSpec(shape, lambda *_: (0,) * n, pipeline_mode=pl.Buffered(1))


def _layer_spec(stack, layer):
    shape = stack.shape[1:]
    return pl.BlockSpec((None,) + shape, lambda *_: (layer,) + (0,) * len(shape), pipeline_mode=pl.Buffered(1))


def _rms(x, g):
    return x * lax.rsqrt(jnp.mean(x * x, axis=-1, keepdims=True) + EPS) * g


def _sigmoid(x):
    return 1.0 / (1.0 + jnp.exp(-x))


def _silu(x):
    return x * _sigmoid(x)


def _bdot(a, b):
    return jnp.dot(a.astype(BF16), b.astype(BF16), preferred_element_type=F32)


def _bdot_nt(a, b):
    return lax.dot_general(a.astype(BF16), b.astype(BF16), NT_DIMS, preferred_element_type=F32)


def _gcb_rows(x, taps, g_ref, w1_ref, wc_ref, bc_ref, w2_ref, z_ref, *, width, cw):
    ffn = bc_ref is not None
    h = _rms(x, g_ref[...]).astype(BF16)
    for c in range(width // cw):
        sl = pl.ds(c * cw, cw)

        def proj(j):
            return jnp.dot(h, w1_ref[:, pl.ds(j * width + c * cw, cw)], preferred_element_type=F32)

        if ffn:
            u, gate = proj(0), proj(1)
        else:
            bg = proj(0)
            u = proj(1) * proj(2)
        um2, um1 = taps(sl, u)
        wc = wc_ref[:, sl]
        conv = wc[0:1] * um2 + wc[1:2] * um1 + wc[2:3] * u
        z = _silu(conv + bc_ref[:, sl]) * gate if ffn else bg * conv
        z_ref[:, sl] = z.astype(BF16)
    return x + jnp.dot(z_ref[...], w2_ref[...], preferred_element_type=F32)


def _gcb_kernel(*refs, ffn, pre, n_tiles, tiles_per_seq, width, cw):
    refs = list(refs)
    o_ref, os_ref, wo_ref = (refs.pop(0), refs.pop(0), refs.pop(0)) if pre else (None, None, None)
    x_ref, xs_ref, sm2_ref, sm1_ref, g_ref, w1_ref, wc_ref = refs[:7]
    refs = refs[7:]
    bc_ref = refs.pop(0) if ffn else None
    w2_ref, out_ref, st_ref, outs_ref, unew_ref, z_ref, zs_ref, carry_ref = refs
    step = pl.program_id(0)
    weights = (g_ref, w1_ref, wc_ref, bc_ref, w2_ref)

    def with_pre(x, o):
        return x + _bdot(o[...], wo_ref[...]) if pre else x

    @pl.when(step < n_tiles)
    def _():
        tm = x_ref.shape[0]

        @pl.when(step % tiles_per_seq == 0)
        def _():
            carry_ref[...] = jnp.zeros_like(carry_ref)

        row = lax.broadcasted_iota(jnp.int32, (tm, cw), 0)

        def taps(sl, u):
            c0, c1 = carry_ref[0:1, sl], carry_ref[1:2, sl]
            um1 = jnp.where(row == 0, c1, pltpu.roll(u, 1, axis=0))
            um2 = jnp.where(row == 0, c0, jnp.where(row == 1, c1, pltpu.roll(u, 2, axis=0)))
            carry_ref[0:2, sl] = u[tm - 2:tm, :]
            return um2, um1

        out_ref[...] = _gcb_rows(with_pre(x_ref[...], o_ref), taps, *weights, z_ref, width=width, cw=cw)
        st_ref[...] = carry_ref[0:2, :]

    @pl.when(step == n_tiles)
    def _():
        def taps(sl, u):
            unew_ref[:, sl] = u
            return sm2_ref[:, sl], sm1_ref[:, sl]

        outs_ref[...] = _gcb_rows(with_pre(xs_ref[...], os_ref), taps, *weights, zs_ref, width=width, cw=cw)


def _gcb_tile(rows):
    for tm in (512, 256, 128, 64, 32, 16, 8):
        if rows % tm == 0:
            return tm
    raise ValueError(f"row count {rows} must be a multiple of 8")


def gated_conv_block(x, xs, state, g, w1, wc, bc, w2, *, seq, pre=None):
    m, d = x.shape
    bd = xs.shape[0]
    width = w2[0].shape[1]
    ffn = bc is not None
    tm = _gcb_tile(seq)
    tps = seq // tm
    n_tiles = m // tm
    cw = 256

    def tile(i):
        return jnp.minimum(i, n_tiles - 1)

    args = [x, xs, state[:, 0], state[:, 1], g.reshape(1, d), w1[0], wc] + (
        [bc.reshape(1, width)] if ffn else []) + [w2[0]]
    in_specs = [pl.BlockSpec((tm, d), lambda i: (tile(i), 0)), _const_spec((bd, d)), _const_spec((bd, width)),
                _const_spec((bd, width)), _const_spec((1, d)), _layer_spec(*w1), _const_spec(wc.shape)] + (
        [_const_spec((1, width))] if ffn else []) + [_layer_spec(*w2)]
    if pre:
        o, os_, wo = pre
        args = [o, os_, wo[0]] + args
        in_specs = [pl.BlockSpec((tm, o.shape[1]), lambda i: (tile(i), 0)), _const_spec(os_.shape),
                    _layer_spec(*wo)] + in_specs
    out, st, outs, unew = pl.pallas_call(
        functools.partial(_gcb_kernel, ffn=ffn, pre=bool(pre), n_tiles=n_tiles, tiles_per_seq=tps, width=width,
                          cw=cw),
        grid=(n_tiles + 1,),
        in_specs=in_specs,
        out_specs=[pl.BlockSpec((tm, d), lambda i: (tile(i), 0)),
                   pl.BlockSpec((None, CONV_TAPS - 1, width), lambda i: (tile(i) // tps, 0, 0)),
                   pl.BlockSpec((bd, d), lambda i: (0, 0)), pl.BlockSpec((bd, width), lambda i: (0, 0))],
        out_shape=[jax.ShapeDtypeStruct((m, d), F32), jax.ShapeDtypeStruct((m // seq, CONV_TAPS - 1, width), F32),
                   jax.ShapeDtypeStruct((bd, d), F32), jax.ShapeDtypeStruct((bd, width), F32)],
        scratch_shapes=[pltpu.VMEM((tm, width), BF16), pltpu.VMEM((bd, width), BF16), pltpu.VMEM((8, width), F32)],
        compiler_params=_params("arbitrary"),
    )(*args)
    return out, st, outs, jnp.stack([state[:, 1], unew], axis=1)


def _lower_bound(rows, layer):
    mx = functools.reduce(jnp.maximum, rows)
    e = [jnp.exp(r - mx) for r in rows]
    tot = functools.reduce(jnp.add, e)
    w = [a / tot for a in e]
    cum = w[0]
    for j in range(1, layer + 1):
        cum = cum + w[j]
    return cum - w[0]


def _split3(x):
    hi = x.astype(BF16)
    r = x - hi.astype(F32)
    mid = r.astype(BF16)
    lo = (r - mid.astype(F32)).astype(BF16)
    return hi, mid, lo


def _level_decay(cum, f, b):
    t = cum.shape[0]
    if b >= 4:
        c3 = cum.reshape(t // (2 * b), 2 * b, HEAD_DIM)
        d = cum - jnp.broadcast_to(c3[:, b - 1:b, :], c3.shape).reshape(t, HEAD_DIM)
        return jnp.exp2(-jnp.abs(d))
    r = lax.broadcasted_iota(jnp.int32, cum.shape, 0) & (2 * b - 1)
    if b == 1:
        return jnp.where(r == 1, f, 1.0)
    return jnp.where(r == 0, pltpu.roll(f, t - 1, axis=0),
                     jnp.where(r == 1, 1.0, jnp.where(r == 2, f, pltpu.roll(f, 1, axis=0) * f)))


def _hgrn_masks(t):
    if t == 2 * HEAD_DIM:
        row = lax.broadcasted_iota(jnp.int32, (t // 2, t), 0)
        col = lax.broadcasted_iota(jnp.int32, (t // 2, t), 1) & (t // 2 - 1)
    else:
        row = lax.broadcasted_iota(jnp.int32, (t, t), 0)
        col = lax.broadcasted_iota(jnp.int32, (t, t), 1)
    return row ^ col, row > col


def _pack_halves(y):
    half = y.shape[0] // 2
    zeros = jnp.zeros((half, HEAD_DIM), y.dtype)
    return jnp.concatenate([jnp.concatenate([y[:half], zeros], axis=1),
                            jnp.concatenate([zeros, y[half:]], axis=1)], axis=0)


def _hgrn_tile(q, k, v, f, cum, st, xor_ts, below):
    t = q.shape[0]
    half = t // 2
    packed = t == 2 * HEAD_DIM
    o = _bdot_nt(q * jnp.exp2(cum), st) + jnp.sum(q * k, axis=-1, keepdims=True) * v
    vb = v.astype(BF16)

    def operands(b):
        x = _level_decay(cum, f, b)
        return (q * x).astype(BF16), (k * x).astype(BF16)

    a = None
    b = half // 2 if packed else half
    while b >= 1:
        yq, yk = operands(b)
        if packed:
            al = lax.dot_general(jnp.concatenate([yq[:half], yq[half:]], axis=1), _pack_halves(yk), NT_DIMS,
                                 preferred_element_type=F32)
        else:
            al = lax.dot_general(yq, yk, NT_DIMS, preferred_element_type=F32)
        a = al if a is None else jnp.where(xor_ts < 2 * b, al, a)
        b //= 2
    a = jnp.where(below, a, 0.0).astype(BF16)
    if packed:
        yq, yk = operands(half)
        cross = lax.dot_general(yq[half:], yk[:half], NT_DIMS, preferred_element_type=F32)
        o_p = jnp.dot(a, _pack_halves(vb), preferred_element_type=F32)
        o = o + jnp.concatenate([o_p[:, :HEAD_DIM], o_p[:, HEAD_DIM:] + _bdot(cross, vb[:half])], axis=0)
    else:
        o = o + jnp.dot(a, vb, preferred_element_type=F32)
    last = cum[t - 1:t]
    upd = lax.dot_general(v.astype(BF16), (k * jnp.exp2(last - cum)).astype(BF16), TN_DIMS,
                          preferred_element_type=F32)
    return o, st * jnp.exp2(last) + upd


def _hgrn_gates(fz, lower):
    f = lower + (1.0 - lower) * _sigmoid(fz)
    k = (1.0 - lower) * _sigmoid(-fz)
    return f, k


def _hgrn_prompt_kernel(x_ref, g_ref, w_ref, lb_ref, gn_ref, wo_ref, out_ref, st_ref,
                        s_ref, q_s, k_s, v_s, gg_s, f_s, cum_s, z_s, *, layer, tiles_per_seq):
    tm, d = x_ref.shape
    nh = d // HEAD_DIM
    t = pl.program_id(0)

    @pl.when(t % tiles_per_seq == 0)
    def _():
        s_ref[...] = jnp.zeros_like(s_ref)

    def proj(j):
        return jnp.dot(hb, w_ref[:, pl.ds(j * d, d)], preferred_element_type=F32)

    x = x_ref[...]
    hb = _rms(x, g_ref[...]).astype(BF16)
    lower = _lower_bound([lb_ref[j:j + 1, :] for j in range(lb_ref.shape[0])], layer)
    f, kk = _hgrn_gates(proj(1), lower)
    k_s[...] = kk
    f_s[...] = f
    row = lax.broadcasted_iota(jnp.int32, (tm, tm), 0)
    col = lax.broadcasted_iota(jnp.int32, (tm, tm), 1)
    tri = jnp.where(row >= col, 1.0, 0.0).astype(BF16)
    cum_s[...] = functools.reduce(jnp.add, [jnp.dot(tri, part, preferred_element_type=F32)
                                            for part in _split3(jnp.log2(f))])
    q_s[...] = _silu(proj(0))
    v_s[...] = proj(2)
    gg_s[...] = proj(3)
    xor_ts, below = _hgrn_masks(tm)
    gn = gn_ref[...]
    for hd in range(nh):
        sl = pl.ds(hd * HEAD_DIM, HEAD_DIM)
        o, st_new = _hgrn_tile(q_s[:, sl], k_s[:, sl], v_s[:, sl], f_s[:, sl], cum_s[:, sl], s_ref[hd],
                               xor_ts, below)
        s_ref[hd] = st_new
        z_s[:, sl] = (_rms(o, gn) * _silu(gg_s[:, sl])).astype(BF16)
    out_ref[...] = x + jnp.dot(z_s[...], wo_ref[...], preferred_element_type=F32)

    @pl.when(t % tiles_per_seq == tiles_per_seq - 1)
    def _():
        for hd in range(nh):
            st_ref[hd] = s_ref[hd].T


def hgrn_prompt(x, g, w_in, lb_raw, g_norm, w_out, *, layer, seq):
    m, d = x.shape
    nh = d // HEAD_DIM
    tm = min(_gcb_tile(seq), 256)
    tps = seq // tm
    return pl.pallas_call(
        functools.partial(_hgrn_prompt_kernel, layer=layer, tiles_per_seq=tps),
        grid=(m // tm,),
        in_specs=[pl.BlockSpec((tm, d), lambda i: (i, 0)), _const_spec((1, d)), _layer_spec(*w_in),
                  _const_spec(lb_raw.shape), _const_spec((1, HEAD_DIM)), _layer_spec(*w_out)],
        out_specs=[pl.BlockSpec((tm, d), lambda i: (i, 0)),
                   pl.BlockSpec((None, nh, HEAD_DIM, HEAD_DIM), lambda i: (i // tps, 0, 0, 0))],
        out_shape=[jax.ShapeDtypeStruct((m, d), F32),
                   jax.ShapeDtypeStruct((m // seq, nh, HEAD_DIM, HEAD_DIM), F32)],
        scratch_shapes=[pltpu.VMEM((nh, HEAD_DIM, HEAD_DIM), F32)] + [pltpu.VMEM((tm, d), F32)] * 6
        + [pltpu.VMEM((tm, d), BF16)],
        compiler_params=_params("arbitrary"),
    )(x, g.reshape(1, d), w_in[0], lb_raw, g_norm.reshape(1, HEAD_DIM), w_out[0])


def _column_replicate(row, n_cols):
    eye = lax.broadcasted_iota(jnp.int32, (HEAD_DIM, HEAD_DIM), 0) == lax.broadcasted_iota(
        jnp.int32, (HEAD_DIM, HEAD_DIM), 1)
    hi = jnp.where(eye, row, 0.0).astype(BF16)
    lo = jnp.where(eye, row - row.astype(BF16).astype(F32), 0.0).astype(BF16)
    rep = jnp.dot(jnp.concatenate([hi, lo], axis=0), jnp.ones((HEAD_DIM, n_cols), BF16), preferred_element_type=F32)
    return rep[:HEAD_DIM] + rep[HEAD_DIM:]


def _hgrn_decode_layer_kernel(x_ref, g_ref, wq_ref, wf_ref, wi_ref, wg_ref, s_ref, lb_ref, gn_ref, wo_ref,
                              out_ref, sn_ref, o_s, *, layer):
    bd = x_ref.shape[0]
    hd = pl.program_id(0)
    x = x_ref[...]
    hb = _rms(x, g_ref[...]).astype(BF16)
    lower_row = _lower_bound([lb_ref[pl.ds(j, 1), :] for j in range(lb_ref.shape[0])], layer)
    lower = _column_replicate(lower_row, bd)
    f_t, k_t = _hgrn_gates(lax.dot_general(wf_ref[...], hb, NT_DIMS, preferred_element_type=F32), lower)
    q_t = _silu(lax.dot_general(wq_ref[...], hb, NT_DIMS, preferred_element_type=F32))
    v = jnp.dot(hb, wi_ref[...], preferred_element_type=F32)
    for b in range(bd):
        col = slice(b, b + 1)
        s_new = (jnp.broadcast_to(f_t[:, col], (HEAD_DIM, HEAD_DIM)) * s_ref[b]
                 + jnp.broadcast_to(k_t[:, col], (HEAD_DIM, HEAD_DIM)) * v[col, :])
        sn_ref[b] = s_new
        o_s[col, :] = jnp.sum(jnp.broadcast_to(q_t[:, col], (HEAD_DIM, HEAD_DIM)) * s_new, axis=0, keepdims=True)
    z = _rms(o_s[...], gn_ref[...]) * _silu(jnp.dot(hb, wg_ref[...], preferred_element_type=F32))

    @pl.when(hd == 0)
    def _():
        out_ref[...] = x

    out_ref[...] += _bdot(z, wo_ref[...])


def hgrn_decode_layer(x, g, w_in, w_in_t, w_out4, slot, states, lb_raw3, g_norm, *, layer):
    bd, d = x.shape
    nh = d // HEAD_DIM
    depth = lb_raw3.shape[0]

    def head_cols(group):
        return pl.BlockSpec((None, d, HEAD_DIM), lambda h: (slot, 0, group * nh + h))

    def head_rows(group):
        return pl.BlockSpec((None, HEAD_DIM, d), lambda h: (slot, group * nh + h, 0))

    state_spec = pl.BlockSpec((bd, None, HEAD_DIM, HEAD_DIM), lambda h: (0, h, 0, 0))
    return pl.pallas_call(
        functools.partial(_hgrn_decode_layer_kernel, layer=layer),
        grid=(nh,),
        in_specs=[_const_spec((bd, d)), _const_spec((1, d)), head_rows(0), head_rows(1), head_cols(2), head_cols(3),
                  pl.BlockSpec((None, bd, None, HEAD_DIM, HEAD_DIM), lambda h: (slot, 0, h, 0, 0)),
                  pl.BlockSpec((None, depth, HEAD_DIM), lambda h: (h, 0, 0)), _const_spec((1, HEAD_DIM)),
                  pl.BlockSpec((None, None, HEAD_DIM, d), lambda h: (slot, h, 0, 0))],
        out_specs=[pl.BlockSpec((bd, d), lambda h: (0, 0)), state_spec],
        out_shape=[jax.ShapeDtypeStruct((bd, d), F32), jax.ShapeDtypeStruct(states.shape[1:], F32)],
        scratch_shapes=[pltpu.VMEM((bd, HEAD_DIM), F32)],
        compiler_params=_params("arbitrary"),
    )(x, g.reshape(1, d), w_in_t, w_in_t, w_in, w_in, states, lb_raw3.transpose(1, 0, 2),
      g_norm.reshape(1, HEAD_DIM), w_out4)


def _rope_tables(pos):
    half = HEAD_DIM // 2
    inv = jnp.exp(-math.log(ROPE_THETA) * jnp.arange(half, dtype=F32) / half)
    ang = pos.astype(F32)[:, None] * inv[None, :]
    cos, sin = jnp.cos(ang), jnp.sin(ang)
    return jnp.concatenate([cos, cos], axis=-1), jnp.concatenate([-sin, sin], axis=-1)


def _qkv_kernel(x_ref, g_ref, w_ref, qn_ref, kn_ref, cos_ref, sin_ref, q_ref, k_ref, v_ref, km_ref):
    d = x_ref.shape[1]
    hb = _rms(x_ref[...], g_ref[...]).astype(BF16)
    cos, sin = cos_ref[...], sin_ref[...]

    def norm_rope(y, gain):
        y = _rms(y, gain)
        return y * cos + pltpu.roll(y, HEAD_DIM // 2, axis=1) * sin

    v_ref[...] = jnp.dot(hb, w_ref[:, pl.ds(2 * d, d)], preferred_element_type=F32)
    for j, (out_ref, gain_ref) in enumerate(((q_ref, qn_ref), (k_ref, kn_ref))):
        y = jnp.dot(hb, w_ref[:, pl.ds(j * d, d)], preferred_element_type=F32)
        for hd in range(d // HEAD_DIM):
            sl = pl.ds(hd * HEAD_DIM, HEAD_DIM)
            yh = norm_rope(y[:, hd * HEAD_DIM:(hd + 1) * HEAD_DIM], gain_ref[...])
            out_ref[:, sl] = yh
            if j == 1:
                km_ref[:, sl] = jnp.mean(yh, axis=0, keepdims=True)


def moba_qkv(x, g, w, qn, kn, cosf, sinf, *, tm):
    m, d = x.shape
    tps = cosf.shape[0] // tm
    return pl.pallas_call(
        _qkv_kernel,
        grid=(m // tm,),
        in_specs=[pl.BlockSpec((tm, d), lambda i: (i, 0)), _const_spec((1, d)), _layer_spec(*w),
                  _const_spec((1, HEAD_DIM)), _const_spec((1, HEAD_DIM)),
                  pl.BlockSpec((tm, HEAD_DIM), lambda i: (i % tps, 0)),
                  pl.BlockSpec((tm, HEAD_DIM), lambda i: (i % tps, 0))],
        out_specs=[pl.BlockSpec((tm, d), lambda i: (i, 0))] * 3 + [pl.BlockSpec((None, 1, d), lambda i: (i, 0, 0))],
        out_shape=[jax.ShapeDtypeStruct((m, d), F32)] * 3 + [jax.ShapeDtypeStruct((m // tm, 1, d), F32)],
        compiler_params=_params("arbitrary"),
    )(x, g.reshape(1, d), w[0], qn.reshape(1, HEAD_DIM), kn.reshape(1, HEAD_DIM), cosf, sinf)


def _top_blocks(gate, n_idx, n_valid, axis):
    neg = -jnp.inf
    g = jnp.where(n_idx < n_valid, gate, neg)
    sel = jnp.zeros(gate.shape, F32)
    big = jnp.int32(2 ** 30)
    for _ in range(MB_TOPK):
        m = jnp.max(g, axis=axis, keepdims=True)
        idx = jnp.min(jnp.where(g == m, n_idx, big), axis=axis, keepdims=True)
        hit = n_idx == idx
        sel = jnp.where(hit, jnp.where(m > neg, 1.0, sel), sel)
        g = jnp.where(hit, neg, g)
    return sel


def _moba_prompt_kernel(q_ref, k_ref, v_ref, km_ref, o_ref, ko_hbm, vo_hbm, kb_ref, vt_ref, sems, *, scale):
    nblk = km_ref.shape[0]
    blk = q_ref.shape[0] // nblk
    b, hd = pl.program_id(0), pl.program_id(1)
    kv_out = [pltpu.make_async_copy(src, dst.at[b, :, hd, :], sems.at[j])
              for j, (src, dst) in enumerate(((k_ref, ko_hbm), (v_ref, vo_hbm)))]
    for cp in kv_out:
        cp.start()
    for n in range(nblk):
        rows = pl.ds(n * blk, blk)
        kb_ref[n] = k_ref[rows, :].astype(BF16)
        vt_ref[:, rows] = v_ref[rows, :].T.astype(BF16)
    km = km_ref[...]
    n_idx = lax.broadcasted_iota(jnp.int32, (nblk, blk), 0)
    causal = lax.broadcasted_iota(jnp.int32, (blk, blk), 0) <= lax.broadcasted_iota(jnp.int32, (blk, blk), 1)
    for i in range(nblk):
        q = q_ref[pl.ds(i * blk, blk), :]
        qb = (q * scale).astype(BF16)

        def scores(n):
            return lax.dot_general(kb_ref[n], qb, NT_DIMS, preferred_element_type=F32)

        s = [None] * i + [jnp.where(causal, scores(i), -jnp.inf)]
        if i > 0:
            gate = lax.dot_general(km, q, NT_DIMS, precision=lax.Precision.HIGHEST, preferred_element_type=F32)
            sel = _top_blocks(gate, n_idx, i, 0)
            for n in range(i):
                s[n] = jnp.where(sel[n:n + 1, :] > 0.5, scores(n), -jnp.inf)
        m = functools.reduce(jnp.maximum, [jnp.max(sn, axis=0, keepdims=True) for sn in s])
        p = [jnp.exp2(sn - m) for sn in s]
        l = functools.reduce(jnp.add, [jnp.sum(pn, axis=0, keepdims=True) for pn in p])
        pcat = jnp.concatenate([pn.astype(BF16) for pn in p], axis=0)
        acc = jnp.dot(vt_ref[:, pl.ds(0, (i + 1) * blk)], pcat, preferred_element_type=F32)
        o_ref[pl.ds(i * blk, blk), :] = (acc / l).T
    for cp in kv_out:
        cp.wait()


def moba_prompt_attention(q, k, v, kmeans, *, batch, seq):
    m, d = q.shape
    nh = d // HEAD_DIM
    nblk = seq // MB_BLOCK
    kv_shape = jax.ShapeDtypeStruct((batch, seq, nh, HEAD_DIM), F32)
    return pl.pallas_call(
        functools.partial(_moba_prompt_kernel, scale=HEAD_DIM ** -0.5 * LOG2_E),
        grid=(batch, nh),
        in_specs=[pl.BlockSpec((seq, HEAD_DIM), lambda b, h: (b, h))] * 3
        + [pl.BlockSpec((None, nblk, HEAD_DIM), lambda b, h: (b, 0, h))],
        out_specs=[pl.BlockSpec((seq, HEAD_DIM), lambda b, h: (b, h)), pl.BlockSpec(memory_space=pl.ANY),
                   pl.BlockSpec(memory_space=pl.ANY)],
        out_shape=[jax.ShapeDtypeStruct((m, d), F32), kv_shape, kv_shape],
        scratch_shapes=[pltpu.VMEM((nblk, MB_BLOCK, HEAD_DIM), BF16), pltpu.VMEM((HEAD_DIM, seq), BF16),
                        pltpu.SemaphoreType.DMA((2,))],
        compiler_params=_params("arbitrary", "arbitrary"),
    )(q, k, v, kmeans)


def _top_block_ids(gate, n_idx, n_take):
    g = gate
    ids = []
    for _ in range(n_take):
        m = jnp.max(g, axis=1, keepdims=True)
        idx = jnp.min(jnp.where(g == m, n_idx, jnp.int32(2 ** 30)), axis=1, keepdims=True)
        ids.append(idx)
        g = jnp.where(n_idx == idx, -jnp.inf, g)
    return ids


def _moba_decode_kernel(pt_ref, q_ref, kn_ref, vn_ref, *rest, scale, n_pages, slot, n_samples):
    k_refs = rest[:n_pages]
    cv_hbm, o_ref, s_ref, vbuf, stat_ref, ids_ref, sems = rest[n_pages:]
    nh = q_ref.shape[0]
    nblk = n_pages // 2
    nsel = min(MB_TOPK, nblk)
    page = k_refs[0].shape[0] // nh
    t = pl.program_id(0)
    par = t % 2

    def v_copy(p, b, h, r, e):
        n = ids_ref[p, h * nsel + r]
        return pltpu.make_async_copy(cv_hbm.at[slot, pt_ref[b, 2 * n + e], :, h, :],
                                     vbuf.at[p, h, r, pl.ds(e * page, page), :], sems.at[p])

    def v_copies(p, b):
        return [v_copy(p, b, h, r, e) for h in range(nh) for r in range(nsel) for e in range(2)]

    @pl.when(t < n_samples)
    def _():
        q = q_ref[...]
        lane = lax.broadcasted_iota(jnp.int32, (nh, HEAD_DIM), 1)
        ones = jnp.ones((HEAD_DIM, HEAD_DIM), BF16)
        gates = jnp.full((nh, HEAD_DIM), -jnp.inf, F32)
        mx = []
        for n in range(nblk):
            ksum, mxn = [], []
            for h in range(nh):
                qh = q[h:h + 1, :] * scale
                ks = jnp.zeros((1, HEAD_DIM), F32)
                mh = jnp.full((1, HEAD_DIM), -jnp.inf, F32)
                for e in range(2):
                    kh = k_refs[2 * n + e][pl.ds(h, page, stride=nh), :]
                    sc = jnp.dot((kh * qh).astype(BF16), ones, preferred_element_type=F32)
                    s_ref[par, n, h, pl.ds(e * page, page), :] = sc
                    ks = ks + jnp.sum(kh, axis=0, keepdims=True)
                    mh = jnp.maximum(mh, jnp.max(sc, axis=0, keepdims=True))
                ksum.append(ks)
                mxn.append(mh)
            mx.append(jnp.concatenate(mxn, axis=0))
            mean = jnp.concatenate(ksum, axis=0) / (2 * page)
            gates = jnp.where(lane == n, jnp.sum(q * mean, axis=1, keepdims=True), gates)
        ids = _top_block_ids(gates, lane, nsel)
        s_own = jnp.sum(q * kn_ref[...], axis=1, keepdims=True) * scale
        m = jnp.broadcast_to(s_own, (nh, HEAD_DIM))
        ids_b = [jnp.broadcast_to(idx, (nh, HEAD_DIM)) for idx in ids]
        for n in range(nblk):
            mx_n = -jnp.inf
            for idx in ids_b:
                mx_n = jnp.where(idx == n, mx[n], mx_n)
            m = jnp.maximum(m, mx_n)
        p_own = jnp.exp2(s_own - m)
        stat_ref[par, 0] = m
        stat_ref[par, 1] = p_own
        stat_ref[par, 2] = p_own * vn_ref[...]
        for r in range(nsel):
            for h in range(nh):
                ids_ref[par, h * nsel + r] = ids[r][h, 0]
        for cp in v_copies(par, t):
            cp.start()

    @pl.when(t >= 1)
    def _():
        prev = 1 - par
        for cp in v_copies(prev, t - 1):
            cp.wait()
        m = stat_ref[prev, 0]
        l_rows, acc_rows = [], []
        for h in range(nh):
            l = jnp.zeros((1, HEAD_DIM), F32)
            acc = jnp.zeros((1, HEAD_DIM), F32)
            for r in range(nsel):
                p = jnp.exp2(s_ref[prev, ids_ref[prev, h * nsel + r], h] - m[h:h + 1, :])
                l = l + jnp.sum(p, axis=0, keepdims=True)
                acc = acc + jnp.sum(p * vbuf[prev, h, r], axis=0, keepdims=True)
            l_rows.append(l)
            acc_rows.append(acc)
        l = stat_ref[prev, 1] + jnp.concatenate(l_rows, axis=0)
        o_ref[...] = (stat_ref[prev, 2] + jnp.concatenate(acc_rows, axis=0)) / l


def moba_decode_attention(q, k_new, v_new, cache_k, cache_v, slot, page_table):
    bd, nh, _ = q.shape
    n_layers, n_phys, page = cache_k.shape[:3]
    n_pages = page_table.shape[1]
    assert 2 * page == MB_BLOCK and n_pages % 2 == 0
    nblk = n_pages // 2
    nsel = min(MB_TOPK, nblk)
    page_rows = page * nh
    ck = cache_k.reshape(n_layers, n_phys, page_rows, HEAD_DIM)

    def kmap(e):
        return lambda t, pt: (slot, pt[jnp.minimum(t, bd - 1), e], 0, 0)

    page_block = (None, None, page_rows, HEAD_DIM)
    vec_k = pl.BlockSpec((None, nh, HEAD_DIM), lambda t, pt: (jnp.minimum(t, bd - 1), 0, 0))
    grid_spec = pltpu.PrefetchScalarGridSpec(
        num_scalar_prefetch=1,
        grid=(bd + 1,),
        in_specs=[vec_k, vec_k, vec_k] + [pl.BlockSpec(page_block, kmap(e)) for e in range(n_pages)]
        + [pl.BlockSpec(memory_space=pl.ANY)],
        out_specs=pl.BlockSpec((None, nh, HEAD_DIM), lambda t, pt: (jnp.maximum(t - 1, 0), 0, 0)),
        scratch_shapes=[pltpu.VMEM((2, nblk, nh, MB_BLOCK, HEAD_DIM), F32),
                        pltpu.VMEM((2, nh, nsel, MB_BLOCK, HEAD_DIM), F32),
                        pltpu.VMEM((2, 3, nh, HEAD_DIM), F32),
                        pltpu.SMEM((2, nh * nsel), jnp.int32),
                        pltpu.SemaphoreType.DMA((2,))],
    )
    return pl.pallas_call(
        functools.partial(_moba_decode_kernel, scale=HEAD_DIM ** -0.5 * LOG2_E, n_pages=n_pages, slot=slot,
                          n_samples=bd),
        grid_spec=grid_spec,
        out_shape=jax.ShapeDtypeStruct((bd, nh, HEAD_DIM), F32),
        compiler_params=_params("arbitrary"),
    )(page_table, q, k_new, v_new, *([ck] * n_pages), cache_v)


def kernel(x_prompt, x_sample, state_shortconv, state_hgrn, cache_k, cache_v, page_table, state_ffn_conv, norm_mix, norm_ffn, w_in_a, w_conv_a, w_out_a, w_in_b, lb_raw, g_norm_b, w_out_b, w_qkv_c, q_norm_c, k_norm_c, w_out_c, w_up, w_ffn_conv, b_ffn_conv, w_down):
    bp, seq, d = x_prompt.shape
    bd = x_sample.shape[0]
    assert x_sample.shape[1] == 1
    depth = norm_mix.shape[0]
    nh = d // HEAD_DIM
    past_len = page_table.shape[1] * cache_k.shape[2]
    assert past_len % MB_BLOCK == 0 and seq % MB_BLOCK == 0

    xp = x_prompt.reshape(bp * seq, d)
    xs = x_sample.reshape(bd, d)
    lb_raw3 = lb_raw.reshape(depth, nh, HEAD_DIM)
    w_in_a, w_out_a, w_in_b, w_out_b, w_qkv_c, w_out_c, w_up, w_down = (
        w.astype(BF16) for w in (w_in_a, w_out_a, w_in_b, w_out_b, w_qkv_c, w_out_c, w_up, w_down))
    w_in_b_t = w_in_b[:, :, :2 * d].transpose(0, 2, 1)
    w_out_b4 = w_out_b.reshape(-1, nh, HEAD_DIM, d)
    outs = {name: [] for name in ("sc_p", "sc_s", "hg_p", "hg_s", "kp", "vp", "ks", "vs", "fc_p", "fc_s")}
    n_conv = n_hgrn = n_moba = 0
    for i in range(depth):
        mixer = i % 3
        pre = None
        if mixer == 0:
            j, n_conv = n_conv, n_conv + 1
            w1, w2 = (w_in_a, j), (w_out_a, j)
            xp, st_p, xs, st_s = gated_conv_block(xp, xs, state_shortconv[j], norm_mix[i], w1, w_conv_a[j], None,
                                                  w2, seq=seq)
            outs["sc_p"].append(st_p)
            outs["sc_s"].append(st_s)
        elif mixer == 1:
            j, n_hgrn = n_hgrn, n_hgrn + 1
            xp, st_p = hgrn_prompt(xp, norm_mix[i], (w_in_b, j), lb_raw, g_norm_b[j], (w_out_b, j), layer=i, seq=seq)
            xs, st_s = hgrn_decode_layer(xs, norm_mix[i], w_in_b, w_in_b_t, w_out_b4, j, state_hgrn, lb_raw3,
                                         g_norm_b[j], layer=i)
            outs["hg_p"].append(st_p)
            outs["hg_s"].append(st_s)
        else:
            j, n_moba = n_moba, n_moba + 1
            wq, wo = (w_qkv_c, j), (w_out_c, j)
            cos_p, sin_p = _rope_tables(jnp.arange(seq, dtype=jnp.int32))
            q, k, v, km = moba_qkv(xp, norm_mix[i], wq, q_norm_c[j], k_norm_c[j], cos_p, sin_p, tm=MB_BLOCK)
            o, k4, v4 = moba_prompt_attention(q, k, v, km.reshape(bp, seq // MB_BLOCK, d), batch=bp, seq=seq)
            cos_s, sin_s = _rope_tables(jnp.full((bd,), past_len, jnp.int32))
            qs, ks, vs, _ = moba_qkv(xs, norm_mix[i], wq, q_norm_c[j], k_norm_c[j], cos_s, sin_s, tm=bd)
            os_ = moba_decode_attention(qs.reshape(bd, nh, HEAD_DIM), ks.reshape(bd, nh, HEAD_DIM),
                                        vs.reshape(bd, nh, HEAD_DIM), cache_k, cache_v, j, page_table)
            pre = (o, os_.reshape(bd, d), wo)
            outs["kp"].append(k4)
            outs["vp"].append(v4)
            outs["ks"].append(ks.reshape(bd, 1, nh, HEAD_DIM))
            outs["vs"].append(vs.reshape(bd, 1, nh, HEAD_DIM))
        w1, w2 = (w_up, i), (w_down, i)
        xp, st_p, xs, st_s = gated_conv_block(xp, xs, state_ffn_conv[i], norm_ffn[i], w1, w_ffn_conv[i],
                                              b_ffn_conv[i], w2, seq=seq, pre=pre)
        outs["fc_p"].append(st_p)
        outs["fc_s"].append(st_s)
    return (xp.reshape(bp, seq, d), xs.reshape(bd, 1, d), jnp.stack(outs["sc_p"]), jnp.stack(outs["sc_s"]),
            jnp.stack(outs["hg_p"]), jnp.stack(outs["hg_s"]), jnp.stack(outs["kp"]), jnp.stack(outs["vp"]),
            jnp.stack(outs["ks"]), jnp.stack(outs["vs"]), jnp.stack(outs["fc_p"]), jnp.stack(outs["fc_s"]))
```

```python
import functools
import math

import jax
import jax.numpy as jnp
from jax import lax
from jax.experimental import pallas as pl
from jax.experimental.pallas import tpu as pltpu

F32 = jnp.float32
BF16 = jnp.bfloat16

EPS = 1e-6
HEAD_DIM = 128
CONV_TAPS = 3
MB_BLOCK = 256
MB_TOPK = 3
ROPE_THETA = 10000.0
LOG2_E = math.log2(math.e)
V7X_VMEM_BYTES = 64 * 1024 * 1024
VMEM_LIMIT = V7X_VMEM_BYTES - 8 * 1024 * 1024

NT_DIMS = (((1,), (1,)), ((), ()))
TN_DIMS = (((0,), (0,)), ((), ()))


def _params(*sem):
    return pltpu.CompilerParams(dimension_semantics=sem, vmem_limit_bytes=VMEM_LIMIT)


def _const_spec(shape):
    n = len(shape)
    return pl.BlockSpec(shape, lambda *_: (0,) * n, pipeline_mode=pl.Buffered(1))


def _layer_spec(stack, layer):
    shape = stack.shape[1:]
    return pl.BlockSpec((None,) + shape, lambda *_: (layer,) + (0,) * len(shape), pipeline_mode=pl.Buffered(1))


def _rms(x, g):
    return x * lax.rsqrt(jnp.mean(x * x, axis=-1, keepdims=True) + EPS) * g


def _sigmoid(x):
    return 1.0 / (1.0 + jnp.exp(-x))


def _silu(x):
    return x * _sigmoid(x)


def _bdot(a, b):
    return jnp.dot(a.astype(BF16), b.astype(BF16), preferred_element_type=F32)


def _bdot_nt(a, b):
    return lax.dot_general(a.astype(BF16), b.astype(BF16), NT_DIMS, preferred_element_type=F32)


def _gcb_rows(x, taps, g_ref, w1_ref, wc_ref, bc_ref, w2_ref, z_ref, *, width, cw):
    ffn = bc_ref is not None
    h = _rms(x, g_ref[...]).astype(BF16)
    for c in range(width // cw):
        sl = pl.ds(c * cw, cw)

        def proj(j):
            return jnp.dot(h, w1_ref[:, pl.ds(j * width + c * cw, cw)], preferred_element_type=F32)

        if ffn:
            u, gate = proj(0), proj(1)
        else:
            bg = proj(0)
            u = proj(1) * proj(2)
        um2, um1 = taps(sl, u)
        wc = wc_ref[:, sl]
        conv = wc[0:1] * um2 + wc[1:2] * um1 + wc[2:3] * u
        z = _silu(conv + bc_ref[:, sl]) * gate if ffn else bg * conv
        z_ref[:, sl] = z.astype(BF16)
    return x + jnp.dot(z_ref[...], w2_ref[...], preferred_element_type=F32)


def _gcb_kernel(*refs, ffn, pre, n_tiles, tiles_per_seq, width, cw):
    refs = list(refs)
    o_ref, os_ref, wo_ref = (refs.pop(0), refs.pop(0), refs.pop(0)) if pre else (None, None, None)
    x_ref, xs_ref, sm2_ref, sm1_ref, g_ref, w1_ref, wc_ref = refs[:7]
    refs = refs[7:]
    bc_ref = refs.pop(0) if ffn else None
    w2_ref, out_ref, st_ref, outs_ref, unew_ref, z_ref, zs_ref, carry_ref = refs
    step = pl.program_id(0)
    weights = (g_ref, w1_ref, wc_ref, bc_ref, w2_ref)

    def with_pre(x, o):
        return x + _bdot(o[...], wo_ref[...]) if pre else x

    @pl.when(step < n_tiles)
    def _():
        tm = x_ref.shape[0]

        @pl.when(step % tiles_per_seq == 0)
        def _():
            carry_ref[...] = jnp.zeros_like(carry_ref)

        row = lax.broadcasted_iota(jnp.int32, (tm, cw), 0)

        def taps(sl, u):
            c0, c1 = carry_ref[0:1, sl], carry_ref[1:2, sl]
            um1 = jnp.where(row == 0, c1, pltpu.roll(u, 1, axis=0))
            um2 = jnp.where(row == 0, c0, jnp.where(row == 1, c1, pltpu.roll(u, 2, axis=0)))
            carry_ref[0:2, sl] = u[tm - 2:tm, :]
            return um2, um1

        out_ref[...] = _gcb_rows(with_pre(x_ref[...], o_ref), taps, *weights, z_ref, width=width, cw=cw)
        st_ref[...] = carry_ref[0:2, :]

    @pl.when(step == n_tiles)
    def _():
        def taps(sl, u):
            unew_ref[:, sl] = u
            return sm2_ref[:, sl], sm1_ref[:, sl]

        outs_ref[...] = _gcb_rows(with_pre(xs_ref[...], os_ref), taps, *weights, zs_ref, width=width, cw=cw)


def _gcb_tile(rows):
    for tm in (512, 256, 128, 64, 32, 16, 8):
        if rows % tm == 0:
            return tm
    raise ValueError(f"row count {rows} must be a multiple of 8")


def gated_conv_block(x, xs, state, g, w1, wc, bc, w2, *, seq, pre=None):
    m, d = x.shape
    bd = xs.shape[0]
    width = w2[0].shape[1]
    ffn = bc is not None
    tm = _gcb_tile(seq)
    tps = seq // tm
    n_tiles = m // tm
    cw = 256

    def tile(i):
        return jnp.minimum(i, n_tiles - 1)

    args = [x, xs, state[:, 0], state[:, 1], g.reshape(1, d), w1[0], wc] + (
        [bc.reshape(1, width)] if ffn else []) + [w2[0]]
    in_specs = [pl.BlockSpec((tm, d), lambda i: (tile(i), 0)), _const_spec((bd, d)), _const_spec((bd, width)),
                _const_spec((bd, width)), _const_spec((1, d)), _layer_spec(*w1), _const_spec(wc.shape)] + (
        [_const_spec((1, width))] if ffn else []) + [_layer_spec(*w2)]
    if pre:
        o, os_, wo = pre
        args = [o, os_, wo[0]] + args
        in_specs = [pl.BlockSpec((tm, o.shape[1]), lambda i: (tile(i), 0)), _const_spec(os_.shape),
                    _layer_spec(*wo)] + in_specs
    out, st, outs, unew = pl.pallas_call(
        functools.partial(_gcb_kernel, ffn=ffn, pre=bool(pre), n_tiles=n_tiles, tiles_per_seq=tps, width=width,
                          cw=cw),
        grid=(n_tiles + 1,),
        in_specs=in_specs,
        out_specs=[pl.BlockSpec((tm, d), lambda i: (tile(i), 0)),
                   pl.BlockSpec((None, CONV_TAPS - 1, width), lambda i: (tile(i) // tps, 0, 0)),
                   pl.BlockSpec((bd, d), lambda i: (0, 0)), pl.BlockSpec((bd, width), lambda i: (0, 0))],
        out_shape=[jax.ShapeDtypeStruct((m, d), F32), jax.ShapeDtypeStruct((m // seq, CONV_TAPS - 1, width), F32),
                   jax.ShapeDtypeStruct((bd, d), F32), jax.ShapeDtypeStruct((bd, width), F32)],
        scratch_shapes=[pltpu.VMEM((tm, width), BF16), pltpu.VMEM((bd, width), BF16), pltpu.VMEM((8, width), F32)],
        compiler_params=_params("arbitrary"),
    )(*args)
    return out, st, outs, jnp.stack([state[:, 1], unew], axis=1)


def _lower_bound(rows, layer):
    mx = functools.reduce(jnp.maximum, rows)
    e = [jnp.exp(r - mx) for r in rows]
    tot = functools.reduce(jnp.add, e)
    w = [a / tot for a in e]
    cum = w[0]
    for j in range(1, layer + 1):
        cum = cum + w[j]
    return cum - w[0]


def _split3(x):
    hi = x.astype(BF16)
    r = x - hi.astype(F32)
    mid = r.astype(BF16)
    lo = (r - mid.astype(F32)).astype(BF16)
    return hi, mid, lo


def _level_decay(cum, f, b):
    t = cum.shape[0]
    if b >= 4:
        c3 = cum.reshape(t // (2 * b), 2 * b, HEAD_DIM)
        d = cum - jnp.broadcast_to(c3[:, b - 1:b, :], c3.shape).reshape(t, HEAD_DIM)
        return jnp.exp2(-jnp.abs(d))
    r = lax.broadcasted_iota(jnp.int32, cum.shape, 0) & (2 * b - 1)
    if b == 1:
        return jnp.where(r == 1, f, 1.0)
    return jnp.where(r == 0, pltpu.roll(f, t - 1, axis=0),
                     jnp.where(r == 1, 1.0, jnp.where(r == 2, f, pltpu.roll(f, 1, axis=0) * f)))


def _hgrn_masks(t):
    if t == 2 * HEAD_DIM:
        row = lax.broadcasted_iota(jnp.int32, (t // 2, t), 0)
        col = lax.broadcasted_iota(jnp.int32, (t // 2, t), 1) & (t // 2 - 1)
    else:
        row = lax.broadcasted_iota(jnp.int32, (t, t), 0)
        col = lax.broadcasted_iota(jnp.int32, (t, t), 1)
    return row ^ col, row > col


def _pack_halves(y):
    half = y.shape[0] // 2
    zeros = jnp.zeros((half, HEAD_DIM), y.dtype)
    return jnp.concatenate([jnp.concatenate([y[:half], zeros], axis=1),
                            jnp.concatenate([zeros, y[half:]], axis=1)], axis=0)


def _hgrn_tile(q, k, v, f, cum, st, xor_ts, below):
    t = q.shape[0]
    half = t // 2
    packed = t == 2 * HEAD_DIM
    o = _bdot_nt(q * jnp.exp2(cum), st) + jnp.sum(q * k, axis=-1, keepdims=True) * v
    vb = v.astype(BF16)

    def operands(b):
        x = _level_decay(cum, f, b)
        return (q * x).astype(BF16), (k * x).astype(BF16)

    a = None
    b = half // 2 if packed else half
    while b >= 1:
        yq, yk = operands(b)
        if packed:
            al = lax.dot_general(jnp.concatenate([yq[:half], yq[half:]], axis=1), _pack_halves(yk), NT_DIMS,
                                 preferred_element_type=F32)
        else:
            al = lax.dot_general(yq, yk, NT_DIMS, preferred_element_type=F32)
        a = al if a is None else jnp.where(xor_ts < 2 * b, al, a)
        b //= 2
    a = jnp.where(below, a, 0.0).astype(BF16)
    if packed:
        yq, yk = operands(half)
        cross = lax.dot_general(yq[half:], yk[:half], NT_DIMS, preferred_element_type=F32)
        o_p = jnp.dot(a, _pack_halves(vb), preferred_element_type=F32)
        o = o + jnp.concatenate([o_p[:, :HEAD_DIM], o_p[:, HEAD_DIM:] + _bdot(cross, vb[:half])], axis=0)
    else:
        o = o + jnp.dot(a, vb, preferred_element_type=F32)
    last = cum[t - 1:t]
    upd = lax.dot_general(v.astype(BF16), (k * jnp.exp2(last - cum)).astype(BF16), TN_DIMS,
                          preferred_element_type=F32)
    return o, st * jnp.exp2(last) + upd


def _hgrn_gates(fz, lower):
    f = lower + (1.0 - lower) * _sigmoid(fz)
    k = (1.0 - lower) * _sigmoid(-fz)
    return f, k


def _hgrn_prompt_kernel(x_ref, g_ref, w_ref, lb_ref, gn_ref, wo_ref, out_ref, st_ref,
                        s_ref, q_s, k_s, v_s, gg_s, f_s, cum_s, z_s, *, layer, tiles_per_seq):
    tm, d = x_ref.shape
    nh = d // HEAD_DIM
    t = pl.program_id(0)

    @pl.when(t % tiles_per_seq == 0)
    def _():
        s_ref[...] = jnp.zeros_like(s_ref)

    def proj(j):
        return jnp.dot(hb, w_ref[:, pl.ds(j * d, d)], preferred_element_type=F32)

    x = x_ref[...]
    hb = _rms(x, g_ref[...]).astype(BF16)
    lower = _lower_bound([lb_ref[j:j + 1, :] for j in range(lb_ref.shape[0])], layer)
    f, kk = _hgrn_gates(proj(1), lower)
    k_s[...] = kk
    f_s[...] = f
    row = lax.broadcasted_iota(jnp.int32, (tm, tm), 0)
    col = lax.broadcasted_iota(jnp.int32, (tm, tm), 1)
    tri = jnp.where(row >= col, 1.0, 0.0).astype(BF16)
    cum_s[...] = functools.reduce(jnp.add, [jnp.dot(tri, part, preferred_element_type=F32)
                                            for part in _split3(jnp.log2(f))])
    q_s[...] = _silu(proj(0))
    v_s[...] = proj(2)
    gg_s[...] = proj(3)
    xor_ts, below = _hgrn_masks(tm)
    gn = gn_ref[...]
    for hd in range(nh):
        sl = pl.ds(hd * HEAD_DIM, HEAD_DIM)
        o, st_new = _hgrn_tile(q_s[:, sl], k_s[:, sl], v_s[:, sl], f_s[:, sl], cum_s[:, sl], s_ref[hd],
                               xor_ts, below)
        s_ref[hd] = st_new
        z_s[:, sl] = (_rms(o, gn) * _silu(gg_s[:, sl])).astype(BF16)
    out_ref[...] = x + jnp.dot(z_s[...], wo_ref[...], preferred_element_type=F32)

    @pl.when(t % tiles_per_seq == tiles_per_seq - 1)
    def _():
        for hd in range(nh):
            st_ref[hd] = s_ref[hd].T


def hgrn_prompt(x, g, w_in, lb_raw, g_norm, w_out, *, layer, seq):
    m, d = x.shape
    nh = d // HEAD_DIM
    tm = min(_gcb_tile(seq), 256)
    tps = seq // tm
    return pl.pallas_call(
        functools.partial(_hgrn_prompt_kernel, layer=layer, tiles_per_seq=tps),
        grid=(m // tm,),
        in_specs=[pl.BlockSpec((tm, d), lambda i: (i, 0)), _const_spec((1, d)), _layer_spec(*w_in),
                  _const_spec(lb_raw.shape), _const_spec((1, HEAD_DIM)), _layer_spec(*w_out)],
        out_specs=[pl.BlockSpec((tm, d), lambda i: (i, 0)),
                   pl.BlockSpec((None, nh, HEAD_DIM, HEAD_DIM), lambda i: (i // tps, 0, 0, 0))],
        out_shape=[jax.ShapeDtypeStruct((m, d), F32),
                   jax.ShapeDtypeStruct((m // seq, nh, HEAD_DIM, HEAD_DIM), F32)],
        scratch_shapes=[pltpu.VMEM((nh, HEAD_DIM, HEAD_DIM), F32)] + [pltpu.VMEM((tm, d), F32)] * 6
        + [pltpu.VMEM((tm, d), BF16)],
        compiler_params=_params("arbitrary"),
    )(x, g.reshape(1, d), w_in[0], lb_raw, g_norm.reshape(1, HEAD_DIM), w_out[0])


def _column_replicate(row, n_cols):
    eye = lax.broadcasted_iota(jnp.int32, (HEAD_DIM, HEAD_DIM), 0) == lax.broadcasted_iota(
        jnp.int32, (HEAD_DIM, HEAD_DIM), 1)
    hi = jnp.where(eye, row, 0.0).astype(BF16)
    lo = jnp.where(eye, row - row.astype(BF16).astype(F32), 0.0).astype(BF16)
    rep = jnp.dot(jnp.concatenate([hi, lo], axis=0), jnp.ones((HEAD_DIM, n_cols), BF16), preferred_element_type=F32)
    return rep[:HEAD_DIM] + rep[HEAD_DIM:]


def _hgrn_decode_layer_kernel(x_ref, g_ref, wq_ref, wf_ref, wi_ref, wg_ref, s_ref, lb_ref, gn_ref, wo_ref,
                              out_ref, sn_ref, o_s, *, layer):
    bd = x_ref.shape[0]
    hd = pl.program_id(0)
    x = x_ref[...]
    hb = _rms(x, g_ref[...]).astype(BF16)
    lower_row = _lower_bound([lb_ref[pl.ds(j, 1), :] for j in range(lb_ref.shape[0])], layer)
    lower = _column_replicate(lower_row, bd)
    f_t, k_t = _hgrn_gates(lax.dot_general(wf_ref[...], hb, NT_DIMS, preferred_element_type=F32), lower)
    q_t = _silu(lax.dot_general(wq_ref[...], hb, NT_DIMS, preferred_element_type=F32))
    v = jnp.dot(hb, wi_ref[...], preferred_element_type=F32)
    for b in range(bd):
        col = slice(b, b + 1)
        s_new = (jnp.broadcast_to(f_t[:, col], (HEAD_DIM, HEAD_DIM)) * s_ref[b]
                 + jnp.broadcast_to(k_t[:, col], (HEAD_DIM, HEAD_DIM)) * v[col, :])
        sn_ref[b] = s_new
        o_s[col, :] = jnp.sum(jnp.broadcast_to(q_t[:, col], (HEAD_DIM, HEAD_DIM)) * s_new, axis=0, keepdims=True)
    z = _rms(o_s[...], gn_ref[...]) * _silu(jnp.dot(hb, wg_ref[...], preferred_element_type=F32))

    @pl.when(hd == 0)
    def _():
        out_ref[...] = x

    out_ref[...] += _bdot(z, wo_ref[...])


def hgrn_decode_layer(x, g, w_in, w_in_t, w_out4, slot, states, lb_raw3, g_norm, *, layer):
    bd, d = x.shape
    nh = d // HEAD_DIM
    depth = lb_raw3.shape[0]

    def head_cols(group):
        return pl.BlockSpec((None, d, HEAD_DIM), lambda h: (slot, 0, group * nh + h))

    def head_rows(group):
        return pl.BlockSpec((None, HEAD_DIM, d), lambda h: (slot, group * nh + h, 0))

    state_spec = pl.BlockSpec((bd, None, HEAD_DIM, HEAD_DIM), lambda h: (0, h, 0, 0))
    return pl.pallas_call(
        functools.partial(_hgrn_decode_layer_kernel, layer=layer),
        grid=(nh,),
        in_specs=[_const_spec((bd, d)), _const_spec((1, d)), head_rows(0), head_rows(1), head_cols(2), head_cols(3),
                  pl.BlockSpec((None, bd, None, HEAD_DIM, HEAD_DIM), lambda h: (slot, 0, h, 0, 0)),
                  pl.BlockSpec((None, depth, HEAD_DIM), lambda h: (h, 0, 0)), _const_spec((1, HEAD_DIM)),
                  pl.BlockSpec((None, None, HEAD_DIM, d), lambda h: (slot, h, 0, 0))],
        out_specs=[pl.BlockSpec((bd, d), lambda h: (0, 0)), state_spec],
        out_shape=[jax.ShapeDtypeStruct((bd, d), F32), jax.ShapeDtypeStruct(states.shape[1:], F32)],
        scratch_shapes=[pltpu.VMEM((bd, HEAD_DIM), F32)],
        compiler_params=_params("arbitrary"),
    )(x, g.reshape(1, d), w_in_t, w_in_t, w_in, w_in, states, lb_raw3.transpose(1, 0, 2),
      g_norm.reshape(1, HEAD_DIM), w_out4)


def _rope_tables(pos):
    half = HEAD_DIM // 2
    inv = jnp.exp(-math.log(ROPE_THETA) * jnp.arange(half, dtype=F32) / half)
    ang = pos.astype(F32)[:, None] * inv[None, :]
    cos, sin = jnp.cos(ang), jnp.sin(ang)
    return jnp.concatenate([cos, cos], axis=-1), jnp.concatenate([-sin, sin], axis=-1)


def _qkv_kernel(x_ref, g_ref, w_ref, qn_ref, kn_ref, cos_ref, sin_ref, q_ref, k_ref, v_ref, km_ref):
    d = x_ref.shape[1]
    hb = _rms(x_ref[...], g_ref[...]).astype(BF16)
    cos, sin = cos_ref[...], sin_ref[...]

    def norm_rope(y, gain):
        y = _rms(y, gain)
        return y * cos + pltpu.roll(y, HEAD_DIM // 2, axis=1) * sin

    v_ref[...] = jnp.dot(hb, w_ref[:, pl.ds(2 * d, d)], preferred_element_type=F32)
    for j, (out_ref, gain_ref) in enumerate(((q_ref, qn_ref), (k_ref, kn_ref))):
        y = jnp.dot(hb, w_ref[:, pl.ds(j * d, d)], preferred_element_type=F32)
        for hd in range(d // HEAD_DIM):
            sl = pl.ds(hd * HEAD_DIM, HEAD_DIM)
            yh = norm_rope(y[:, hd * HEAD_DIM:(hd + 1) * HEAD_DIM], gain_ref[...])
            out_ref[:, sl] = yh
            if j == 1:
                km_ref[:, sl] = jnp.mean(yh, axis=0, keepdims=True)


def moba_qkv(x, g, w, qn, kn, cosf, sinf, *, tm):
    m, d = x.shape
    tps = cosf.shape[0] // tm
    return pl.pallas_call(
        _qkv_kernel,
        grid=(m // tm,),
        in_specs=[pl.BlockSpec((tm, d), lambda i: (i, 0)), _const_spec((1, d)), _layer_spec(*w),
                  _const_spec((1, HEAD_DIM)), _const_spec((1, HEAD_DIM)),
                  pl.BlockSpec((tm, HEAD_DIM), lambda i: (i % tps, 0)),
                  pl.BlockSpec((tm, HEAD_DIM), lambda i: (i % tps, 0))],
        out_specs=[pl.BlockSpec((tm, d), lambda i: (i, 0))] * 3 + [pl.BlockSpec((None, 1, d), lambda i: (i, 0, 0))],
        out_shape=[jax.ShapeDtypeStruct((m, d), F32)] * 3 + [jax.ShapeDtypeStruct((m // tm, 1, d), F32)],
        compiler_params=_params("arbitrary"),
    )(x, g.reshape(1, d), w[0], qn.reshape(1, HEAD_DIM), kn.reshape(1, HEAD_DIM), cosf, sinf)


def _top_blocks(gate, n_idx, n_valid, axis):
    neg = -jnp.inf
    g = jnp.where(n_idx < n_valid, gate, neg)
    sel = jnp.zeros(gate.shape, F32)
    big = jnp.int32(2 ** 30)
    for _ in range(MB_TOPK):
        m = jnp.max(g, axis=axis, keepdims=True)
        idx = jnp.min(jnp.where(g == m, n_idx, big), axis=axis, keepdims=True)
        hit = n_idx == idx
        sel = jnp.where(hit, jnp.where(m > neg, 1.0, sel), sel)
        g = jnp.where(hit, neg, g)
    return sel


def _moba_prompt_kernel(q_ref, k_ref, v_ref, km_ref, o_ref, ko_hbm, vo_hbm, kb_ref, vt_ref, sems, *, scale):
    nblk = km_ref.shape[0]
    blk = q_ref.shape[0] // nblk
    b, hd = pl.program_id(0), pl.program_id(1)
    kv_out = [pltpu.make_async_copy(src, dst.at[b, :, hd, :], sems.at[j])
              for j, (src, dst) in enumerate(((k_ref, ko_hbm), (v_ref, vo_hbm)))]
    for cp in kv_out:
        cp.start()
    for n in range(nblk):
        rows = pl.ds(n * blk, blk)
        kb_ref[n] = k_ref[rows, :].astype(BF16)
        vt_ref[:, rows] = v_ref[rows, :].T.astype(BF16)
    km = km_ref[...]
    n_idx = lax.broadcasted_iota(jnp.int32, (nblk, blk), 0)
    causal = lax.broadcasted_iota(jnp.int32, (blk, blk), 0) <= lax.broadcasted_iota(jnp.int32, (blk, blk), 1)
    for i in range(nblk):
        q = q_ref[pl.ds(i * blk, blk), :]
        qb = (q * scale).astype(BF16)

        def scores(n):
            return lax.dot_general(kb_ref[n], qb, NT_DIMS, preferred_element_type=F32)

        s = [None] * i + [jnp.where(causal, scores(i), -jnp.inf)]
        if i > 0:
            gate = lax.dot_general(km, q, NT_DIMS, precision=lax.Precision.HIGHEST, preferred_element_type=F32)
            sel = _top_blocks(gate, n_idx, i, 0)
            for n in range(i):
                s[n] = jnp.where(sel[n:n + 1, :] > 0.5, scores(n), -jnp.inf)
        m = functools.reduce(jnp.maximum, [jnp.max(sn, axis=0, keepdims=True) for sn in s])
        p = [jnp.exp2(sn - m) for sn in s]
        l = functools.reduce(jnp.add, [jnp.sum(pn, axis=0, keepdims=True) for pn in p])
        pcat = jnp.concatenate([pn.astype(BF16) for pn in p], axis=0)
        acc = jnp.dot(vt_ref[:, pl.ds(0, (i + 1) * blk)], pcat, preferred_element_type=F32)
        o_ref[pl.ds(i * blk, blk), :] = (acc / l).T
    for cp in kv_out:
        cp.wait()


def moba_prompt_attention(q, k, v, kmeans, *, batch, seq):
    m, d = q.shape
    nh = d // HEAD_DIM
    nblk = seq // MB_BLOCK
    kv_shape = jax.ShapeDtypeStruct((batch, seq, nh, HEAD_DIM), F32)
    return pl.pallas_call(
        functools.partial(_moba_prompt_kernel, scale=HEAD_DIM ** -0.5 * LOG2_E),
        grid=(batch, nh),
        in_specs=[pl.BlockSpec((seq, HEAD_DIM), lambda b, h: (b, h))] * 3
        + [pl.BlockSpec((None, nblk, HEAD_DIM), lambda b, h: (b, 0, h))],
        out_specs=[pl.BlockSpec((seq, HEAD_DIM), lambda b, h: (b, h)), pl.BlockSpec(memory_space=pl.ANY),
                   pl.BlockSpec(memory_space=pl.ANY)],
        out_shape=[jax.ShapeDtypeStruct((m, d), F32), kv_shape, kv_shape],
        scratch_shapes=[pltpu.VMEM((nblk, MB_BLOCK, HEAD_DIM), BF16), pltpu.VMEM((HEAD_DIM, seq), BF16),
                        pltpu.SemaphoreType.DMA((2,))],
        compiler_params=_params("arbitrary", "arbitrary"),
    )(q, k, v, kmeans)


def _top_block_ids(gate, n_idx, n_take):
    g = gate
    ids = []
    for _ in range(n_take):
        m = jnp.max(g, axis=1, keepdims=True)
        idx = jnp.min(jnp.where(g == m, n_idx, jnp.int32(2 ** 30)), axis=1, keepdims=True)
        ids.append(idx)
        g = jnp.where(n_idx == idx, -jnp.inf, g)
    return ids


def _moba_decode_kernel(pt_ref, q_ref, kn_ref, vn_ref, *rest, scale, n_pages, slot, n_samples):
    k_refs = rest[:n_pages]
    cv_hbm, o_ref, s_ref, vbuf, stat_ref, ids_ref, sems = rest[n_pages:]
    nh = q_ref.shape[0]
    nblk = n_pages // 2
    nsel = min(MB_TOPK, nblk)
    page = k_refs[0].shape[0] // nh
    t = pl.program_id(0)
    par = t % 2

    def v_copy(p, b, h, r, e):
        n = ids_ref[p, h * nsel + r]
        return pltpu.make_async_copy(cv_hbm.at[slot, pt_ref[b, 2 * n + e], :, h, :],
                                     vbuf.at[p, h, r, pl.ds(e * page, page), :], sems.at[p])

    def v_copies(p, b):
        return [v_copy(p, b, h, r, e) for h in range(nh) for r in range(nsel) for e in range(2)]

    @pl.when(t < n_samples)
    def _():
        q = q_ref[...]
        lane = lax.broadcasted_iota(jnp.int32, (nh, HEAD_DIM), 1)
        ones = jnp.ones((HEAD_DIM, HEAD_DIM), BF16)
        gates = jnp.full((nh, HEAD_DIM), -jnp.inf, F32)
        mx = []
        for n in range(nblk):
            ksum, mxn = [], []
            for h in range(nh):
                qh = q[h:h + 1, :] * scale
                ks = jnp.zeros((1, HEAD_DIM), F32)
                mh = jnp.full((1, HEAD_DIM), -jnp.inf, F32)
                for e in range(2):
                    kh = k_refs[2 * n + e][pl.ds(h, page, stride=nh), :]
                    sc = jnp.dot((kh * qh).astype(BF16), ones, preferred_element_type=F32)
                    s_ref[par, n, h, pl.ds(e * page, page), :] = sc
                    ks = ks + jnp.sum(kh, axis=0, keepdims=True)
                    mh = jnp.maximum(mh, jnp.max(sc, axis=0, keepdims=True))
                ksum.append(ks)
                mxn.append(mh)
            mx.append(jnp.concatenate(mxn, axis=0))
            mean = jnp.concatenate(ksum, axis=0) / (2 * page)
            gates = jnp.where(lane == n, jnp.sum(q * mean, axis=1, keepdims=True), gates)
        ids = _top_block_ids(gates, lane, nsel)
        s_own = jnp.sum(q * kn_ref[...], axis=1, keepdims=True) * scale
        m = jnp.broadcast_to(s_own, (nh, HEAD_DIM))
        ids_b = [jnp.broadcast_to(idx, (nh, HEAD_DIM)) for idx in ids]
        for n in range(nblk):
            mx_n = -jnp.inf
            for idx in ids_b:
                mx_n = jnp.where(idx == n, mx[n], mx_n)
            m = jnp.maximum(m, mx_n)
        p_own = jnp.exp2(s_own - m)
        stat_ref[par, 0] = m
        stat_ref[par, 1] = p_own
        stat_ref[par, 2] = p_own * vn_ref[...]
        for r in range(nsel):
            for h in range(nh):
                ids_ref[par, h * nsel + r] = ids[r][h, 0]
        for i, cp in enumerate(v_copies(par, t)):
            cp.start(priority=i % 2)

    @pl.when(t >= 1)
    def _():
        prev = 1 - par
        for cp in v_copies(prev, t - 1):
            cp.wait()
        m = stat_ref[prev, 0]
        l_rows, acc_rows = [], []
        for h in range(nh):
            l = jnp.zeros((1, HEAD_DIM), F32)
            acc = jnp.zeros((1, HEAD_DIM), F32)
            for r in range(nsel):
                p = jnp.exp2(s_ref[prev, ids_ref[prev, h * nsel + r], h] - m[h:h + 1, :])
                l = l + jnp.sum(p, axis=0, keepdims=True)
                acc = acc + jnp.sum(p * vbuf[prev, h, r], axis=0, keepdims=True)
            l_rows.append(l)
            acc_rows.append(acc)
        l = stat_ref[prev, 1] + jnp.concatenate(l_rows, axis=0)
        o_ref[...] = (stat_ref[prev, 2] + jnp.concatenate(acc_rows, axis=0)) / l


def moba_decode_attention(q, k_new, v_new, cache_k, cache_v, slot, page_table):
    bd, nh, _ = q.shape
    n_layers, n_phys, page = cache_k.shape[:3]
    n_pages = page_table.shape[1]
    assert 2 * page == MB_BLOCK and n_pages % 2 == 0
    nblk = n_pages // 2
    nsel = min(MB_TOPK, nblk)
    page_rows = page * nh
    ck = cache_k.reshape(n_layers, n_phys, page_rows, HEAD_DIM)

    def kmap(e):
        return lambda t, pt: (slot, pt[jnp.minimum(t, bd - 1), e], 0, 0)

    page_block = (None, None, page_rows, HEAD_DIM)
    vec_k = pl.BlockSpec((None, nh, HEAD_DIM), lambda t, pt: (jnp.minimum(t, bd - 1), 0, 0))
    grid_spec = pltpu.PrefetchScalarGridSpec(
        num_scalar_prefetch=1,
        grid=(bd + 1,),
        in_specs=[vec_k, vec_k, vec_k] + [pl.BlockSpec(page_block, kmap(e)) for e in range(n_pages)]
        + [pl.BlockSpec(memory_space=pl.ANY)],
        out_specs=pl.BlockSpec((None, nh, HEAD_DIM), lambda t, pt: (jnp.maximum(t - 1, 0), 0, 0)),
        scratch_shapes=[pltpu.VMEM((2, nblk, nh, MB_BLOCK, HEAD_DIM), F32),
                        pltpu.VMEM((2, nh, nsel, MB_BLOCK, HEAD_DIM), F32),
                        pltpu.VMEM((2, 3, nh, HEAD_DIM), F32),
                        pltpu.SMEM((2, nh * nsel), jnp.int32),
                        pltpu.SemaphoreType.DMA((2,))],
    )
    return pl.pallas_call(
        functools.partial(_moba_decode_kernel, scale=HEAD_DIM ** -0.5 * LOG2_E, n_pages=n_pages, slot=slot,
                          n_samples=bd),
        grid_spec=grid_spec,
        out_shape=jax.ShapeDtypeStruct((bd, nh, HEAD_DIM), F32),
        compiler_params=_params("arbitrary"),
    )(page_table, q, k_new, v_new, *([ck] * n_pages), cache_v)


def kernel(x_prompt, x_sample, state_shortconv, state_hgrn, cache_k, cache_v, page_table, state_ffn_conv, norm_mix, norm_ffn, w_in_a, w_conv_a, w_out_a, w_in_b, lb_raw, g_norm_b, w_out_b, w_qkv_c, q_norm_c, k_norm_c, w_out_c, w_up, w_ffn_conv, b_ffn_conv, w_down):
    bp, seq, d = x_prompt.shape
    bd = x_sample.shape[0]
    assert x_sample.shape[1] == 1
    depth = norm_mix.shape[0]
    nh = d // HEAD_DIM
    past_len = page_table.shape[1] * cache_k.shape[2]
    assert past_len % MB_BLOCK == 0 and seq % MB_BLOCK == 0

    xp = x_prompt.reshape(bp * seq, d)
    xs = x_sample.reshape(bd, d)
    lb_raw3 = lb_raw.reshape(depth, nh, HEAD_DIM)
    w_in_a, w_out_a, w_in_b, w_out_b, w_qkv_c, w_out_c, w_up, w_down = (
        w.astype(BF16) for w in (w_in_a, w_out_a, w_in_b, w_out_b, w_qkv_c, w_out_c, w_up, w_down))
    w_in_b_t = w_in_b[:, :, :2 * d].transpose(0, 2, 1)
    w_out_b4 = w_out_b.reshape(-1, nh, HEAD_DIM, d)
    outs = {name: [] for name in ("sc_p", "sc_s", "hg_p", "hg_s", "kp", "vp", "ks", "vs", "fc_p", "fc_s")}
    n_conv = n_hgrn = n_moba = 0
    for i in range(depth):
        mixer = i % 3
        pre = None
        if mixer == 0:
            j, n_conv = n_conv, n_conv + 1
            w1, w2 = (w_in_a, j), (w_out_a, j)
            xp, st_p, xs, st_s = gated_conv_block(xp, xs, state_shortconv[j], norm_mix[i], w1, w_conv_a[j], None,
                                                  w2, seq=seq)
            outs["sc_p"].append(st_p)
            outs["sc_s"].append(st_s)
        elif mixer == 1:
            j, n_hgrn = n_hgrn, n_hgrn + 1
            xp, st_p = hgrn_prompt(xp, norm_mix[i], (w_in_b, j), lb_raw, g_norm_b[j], (w_out_b, j), layer=i, seq=seq)
            xs, st_s = hgrn_decode_layer(xs, norm_mix[i], w_in_b, w_in_b_t, w_out_b4, j, state_hgrn, lb_raw3,
                                         g_norm_b[j], layer=i)
            outs["hg_p"].append(st_p)
            outs["hg_s"].append(st_s)
        else:
            j, n_moba = n_moba, n_moba + 1
            wq, wo = (w_qkv_c, j), (w_out_c, j)
            cos_p, sin_p = _rope_tables(jnp.arange(seq, dtype=jnp.int32))
            q, k, v, km = moba_qkv(xp, norm_mix[i], wq, q_norm_c[j], k_norm_c[j], cos_p, sin_p, tm=MB_BLOCK)
            o, k4, v4 = moba_prompt_attention(q, k, v, km.reshape(bp, seq // MB_BLOCK, d), batch=bp, seq=seq)
            cos_s, sin_s = _rope_tables(jnp.full((bd,), past_len, jnp.int32))
            qs, ks, vs, _ = moba_qkv(xs, norm_mix[i], wq, q_norm_c[j], k_norm_c[j], cos_s, sin_s, tm=bd)
            os_ = moba_decode_attention(qs.reshape(bd, nh, HEAD_DIM), ks.reshape(bd, nh, HEAD_DIM),
                                        vs.reshape(bd, nh, HEAD_DIM), cache_k, cache_v, j, page_table)
            pre = (o, os_.reshape(bd, d), wo)
            outs["kp"].append(k4)
            outs["vp"].append(v4)
            outs["ks"].append(ks.reshape(bd, 1, nh, HEAD_DIM))
            outs["vs"].append(vs.reshape(bd, 1, nh, HEAD_DIM))
        w1, w2 = (w_up, i), (w_down, i)
        xp, st_p, xs, st_s = gated_conv_block(xp, xs, state_ffn_conv[i], norm_ffn[i], w1, w_ffn_conv[i],
                                              b_ffn_conv[i], w2, seq=seq, pre=pre)
        outs["fc_p"].append(st_p)
        outs["fc_s"].append(st_s)
    return (xp.reshape(bp, seq, d), xs.reshape(bd, 1, d), jnp.stack(outs["sc_p"]), jnp.stack(outs["sc_s"]),
            jnp.stack(outs["hg_p"]), jnp.stack(outs["hg_s"]), jnp.stack(outs["kp"]), jnp.stack(outs["vp"]),
            jnp.stack(outs["ks"]), jnp.stack(outs["vs"]), jnp.stack(outs["fc_p"]), jnp.stack(outs["fc_s"]))
```
